```python
import math
import jax, jax.numpy as jnp
from jax import lax
import numpy as np

D_MODEL = 2048
BATCH = 32
SEQ = 256
DEPTH = 2
DEC_BATCH = 8
DEC_SEQ = 2048
PAST_LEN = 256

GRID_W = 64
N_DIRS = 2
MIX_WIDTH = D_MODEL
HG_DK = 128
HG_DV = 128
HG_HEADS = MIX_WIDTH // 2 // HG_DV
HG_QK_DIM = HG_HEADS * HG_DK
HG_V_DIM = HG_HEADS * HG_DV
GD_DK = 128
GD_DV = 128
GD_HEADS = MIX_WIDTH // 2 // GD_DV
GD_QK_DIM = GD_HEADS * GD_DK
GD_V_DIM = GD_HEADS * GD_DV
GD_CONV_DIM = 2 * GD_QK_DIM + GD_V_DIM
CONV_K = 5
HG_CHUNK = 32
GD_CHUNK = 64
D_FF = 5504
N_MOD = 9
EPS = 1e-6
SPLIT_SIZES = (HG_QK_DIM, HG_V_DIM, HG_QK_DIM, HG_QK_DIM, HG_V_DIM,
               GD_QK_DIM, GD_QK_DIM, GD_V_DIM, GD_V_DIM,
               GD_HEADS, GD_HEADS, GD_HEADS, GD_HEADS)
IN_COLS = 3 * HG_QK_DIM + 2 * HG_V_DIM + 2 * GD_QK_DIM + 2 * GD_V_DIM + 4 * GD_HEADS

kernel_name = 'hymba_hgrn2_gdn_macaron_dit_step'


def _rms_norm(x, w):
    x32 = x.astype(jnp.float32)
    y = x32 * lax.rsqrt(jnp.mean(x32 * x32, axis=-1, keepdims=True) + EPS) * w.astype(jnp.float32)
    return y.astype(x.dtype)


def _modulation(cond, w, b):
    return (jax.nn.silu(cond) @ w + b).reshape(cond.shape[0], N_MOD, -1)


def _swiglu(h, w_in, w_out):
    gate, up = jnp.split(h @ w_in, 2, axis=-1)
    return (jax.nn.silu(gate) * up) @ w_out


def _heads(x, n):
    b, t, _ = x.shape
    return x.reshape(b, t, n, -1).transpose(0, 2, 1, 3)


def _merge(x):
    b, h, t, d = x.shape
    return x.transpose(0, 2, 1, 3).reshape(b, t, h * d)


def _l2norm(x):
    return x * lax.rsqrt(jnp.sum(x * x, axis=-1, keepdims=True) + EPS)


def _head_norm_gate(o, w, gate, dtype):
    on = o * lax.rsqrt(jnp.mean(o * o, axis=-1, keepdims=True) + EPS) * w.astype(jnp.float32)
    return (_merge(on) * jax.nn.silu(gate.astype(jnp.float32))).astype(dtype)


def _short_conv(x, w, on_grid):
    b, t, ch = x.shape
    if on_grid:
        rows = t // GRID_W
        x = x.reshape(b * rows, GRID_W, ch)
    y = lax.conv_general_dilated(x, w[:, None, :].astype(x.dtype), window_strides=(1,),
                                 padding=[(CONV_K // 2, CONV_K // 2)],
                                 dimension_numbers=('NWC', 'WIO', 'NWC'),
                                 feature_group_count=ch)
    return y.reshape(b, t, ch)


def _to_chunks(a, c):
    b, h, t = a.shape[:3]
    return jnp.moveaxis(a.reshape(b, h, t // c, c, *a.shape[3:]), 2, 0)


def _from_chunks(a):
    n, b, h, c = a.shape[:4]
    return jnp.moveaxis(a, 0, 2).reshape(b, h, n * c, *a.shape[4:])


def _hgrn2_chunk_scan(q, k, v, log_f, s0):
    c = HG_CHUNK
    tri = jnp.tril(jnp.ones((c, c), bool))

    def step(s, blk):
        qc, kc, vc, gc = blk
        bcum = jnp.cumsum(gc, axis=-2)
        diff = bcum[..., :, None, :] - bcum[..., None, :, :]
        decay = jnp.exp(jnp.where(tri[:, :, None], diff, -jnp.inf))
        att = jnp.einsum('bhtd,bhsd,bhtsd->bhts', qc, kc, decay)
        o = (jnp.einsum('bhts,bhsv->bhtv', att, vc)
             + jnp.einsum('bhtd,bhdv->bhtv', qc * jnp.exp(bcum), s))
        b_last = bcum[..., -1, :]
        s = (jnp.exp(b_last)[..., None] * s
             + jnp.einsum('bhsd,bhsv->bhdv', kc * jnp.exp(b_last[..., None, :] - bcum), vc))
        return s, o

    xs = tuple(_to_chunks(a, c) for a in (q, k, v, log_f))
    s_fin, o = lax.scan(step, s0, xs)
    return _from_chunks(o), s_fin


def _gdn_chunk_scan(q, k, v, g, beta, s0):
    c = GD_CHUNK
    b, h, t, _ = q.shape
    dv = v.shape[-1]
    n = t // c
    q, k, v = (a.reshape(b, h, n, c, a.shape[-1]) for a in (q, k, v))
    g, beta = (a.reshape(b, h, n, c) for a in (g, beta))
    gam = jnp.cumsum(g, axis=-1)
    incl = jnp.tril(jnp.ones((c, c), bool))
    strict = jnp.tril(jnp.ones((c, c), bool), -1)
    decay = jnp.exp(jnp.where(incl, gam[..., :, None] - gam[..., None, :], -jnp.inf))
    kb = k * beta[..., None]
    a_low = jnp.where(strict, jnp.einsum('bhntd,bhnsd->bhnts', kb, k) * decay, 0.0)
    rhs = jnp.concatenate([v * beta[..., None], kb * jnp.exp(gam)[..., None]], axis=-1)
    sol = lax.linalg.triangular_solve(a_low + jnp.eye(c, dtype=q.dtype), rhs, left_side=True,
                                      lower=True, unit_diagonal=True)
    u, w = sol[..., :dv], sol[..., dv:]
    a_qk = jnp.einsum('bhntd,bhnsd->bhnts', q, k) * decay

    def step(s, blk):
        qc, kc, uc, wc, ac, gc = blk
        v_new = uc - jnp.einsum('bhtd,bhdv->bhtv', wc, s)
        o = (jnp.einsum('bhtd,bhdv->bhtv', qc * jnp.exp(gc)[..., None], s)
             + jnp.einsum('bhts,bhsv->bhtv', ac, v_new))
        g_last = gc[..., -1]
        s = (jnp.exp(g_last)[..., None, None] * s
             + jnp.einsum('bhsd,bhsv->bhdv', kc * jnp.exp(g_last[..., None] - gc)[..., None], v_new))
        return s, o

    xs = tuple(jnp.moveaxis(a, 2, 0) for a in (q, k, u, w, a_qk, gam))
    s_fin, o = lax.scan(step, s0, xs)
    return jnp.moveaxis(o, 0, 2).reshape(b, h, t, dv), s_fin


def _scan_dir(fn, arrays, s0, reverse):
    if reverse:
        arrays = [jnp.flip(a, axis=2) for a in arrays]
    o, s = fn(*arrays, s0)
    if reverse:
        o = jnp.flip(o, axis=2)
    return o, s


def _hgrn2_lower_bounds(hg_lower_bounds):
    cs = jnp.cumsum(jax.nn.softmax(hg_lower_bounds.astype(jnp.float32), axis=1), axis=1)
    return cs - cs[:, :1]


def _mixer(h, p, lbs, l, s_hg0, s_gd0, on_grid):
    f32 = jnp.float32
    dt = h.dtype
    offs = [sum(SPLIT_SIZES[:i + 1]) for i in range(len(SPLIT_SIZES) - 1)]
    (hq, hi, hf_f, hf_b, hgate, gq, gk, gv, ggate,
     ga_f, ga_b, gb_f, gb_b) = jnp.split(h @ p['w_in'][l], offs, axis=-1)

    q = _heads(jax.nn.silu(hq.astype(f32)), HG_HEADS) * HG_DK ** -0.5
    v = _heads(hi.astype(f32), HG_HEADS)
    o_hg = 0.0
    s_hg = []
    for d, f_raw in ((0, hf_f), (1, hf_b)):
        lb = lbs[d, l]
        fr = f_raw.astype(f32)
        log_f = _heads(jnp.logaddexp(jnp.log(lb), jnp.log1p(-lb) + jax.nn.log_sigmoid(fr)), HG_HEADS)
        k = _heads((1.0 - lb) * jax.nn.sigmoid(-fr), HG_HEADS)
        o_d, s_d = _scan_dir(_hgrn2_chunk_scan, [q, k, v, log_f], s_hg0[:, d].astype(f32), d == 1)
        o_hg = o_hg + o_d
        s_hg.append(s_d)
    out_hg = _head_norm_gate(o_hg, p['hg_norm_w'][l], hgate, dt)

    qkv = jax.nn.silu(_short_conv(jnp.concatenate([gq, gk, gv], axis=-1), p['gd_conv_w'][l], on_grid))
    cq, ck, cv = jnp.split(qkv.astype(f32), [GD_QK_DIM, 2 * GD_QK_DIM], axis=-1)
    q = _l2norm(_heads(cq, GD_HEADS)) * GD_DK ** -0.5
    k = _l2norm(_heads(ck, GD_HEADS))
    v = _heads(cv, GD_HEADS)
    o_gd = 0.0
    s_gd = []
    for d, a_raw, b_raw in ((0, ga_f, gb_f), (1, ga_b, gb_b)):
        a_log = p['gd_A_log'][l, d].astype(f32)
        dt_bias = p['gd_dt_bias'][l, d].astype(f32)
        g = (-jnp.exp(a_log) * jax.nn.softplus(a_raw.astype(f32) + dt_bias)).transpose(0, 2, 1)
        beta = jax.nn.sigmoid(b_raw.astype(f32)).transpose(0, 2, 1)
        o_d, s_d = _scan_dir(_gdn_chunk_scan, [q, k, v, g, beta], s_gd0[:, d].astype(f32), d == 1)
        o_gd = o_gd + o_d
        s_gd.append(s_d)
    out_gd = _head_norm_gate(o_gd, p['gd_norm_w'][l], ggate, dt)

    out = jnp.concatenate([out_hg, out_gd], axis=-1) @ p['w_out'][l]
    return out, jnp.stack(s_hg, axis=1), jnp.stack(s_gd, axis=1)


def _layer(x, mod, p, lbs, l, s_hg0, s_gd0, on_grid):
    m = [mod[:, i, None, :] for i in range(N_MOD)]
    h = _rms_norm(x, p['norm_w'][l, 0]) * (1 + m[1]) + m[0]
    x = x + 0.5 * m[2] * _swiglu(h, p['ffn_w_in'][l, 0], p['ffn_w_out'][l, 0])
    h = _rms_norm(x, p['norm_w'][l, 1]) * (1 + m[4]) + m[3]
    o, s_hg, s_gd = _mixer(h, p, lbs, l, s_hg0, s_gd0, on_grid)
    x = x + m[5] * o
    h = _rms_norm(x, p['norm_w'][l, 2]) * (1 + m[7]) + m[6]
    x = x + 0.5 * m[8] * _swiglu(h, p['ffn_w_in'][l, 1], p['ffn_w_out'][l, 1])
    return x, s_hg, s_gd


def setup_inputs(seed: int = 0) -> dict:
    key = jax.random.key(seed)
    ks = jax.random.split(key, 20)
    f32 = jnp.float32

    def nrm(k, shape, s):
        return jax.random.normal(k, shape, f32) * s

    x_prompt = nrm(ks[0], (BATCH, SEQ, D_MODEL), 1.0)
    x_sample = nrm(ks[1], (DEC_BATCH, DEC_SEQ, D_MODEL), 1.0)
    c = nrm(ks[2], (DEC_BATCH, D_MODEL), 1.0)
    state_hgrn2 = nrm(ks[3], (DEC_BATCH, DEPTH, N_DIRS, HG_HEADS, HG_DK, HG_DV), 0.5)
    state_gdn = nrm(ks[4], (DEC_BATCH, DEPTH, N_DIRS, GD_HEADS, GD_DK, GD_DV), 0.1)
    c_ctx = nrm(ks[5], (D_MODEL,), 1.0)
    norm_w = 1.0 + nrm(ks[6], (DEPTH, 3, D_MODEL), 0.01)
    w_mod = nrm(ks[7], (DEPTH, D_MODEL, N_MOD * D_MODEL), D_MODEL ** -0.5)
    b_mod = nrm(ks[8], (DEPTH, N_MOD * D_MODEL), 0.02)
    ffn_w_in = nrm(ks[9], (DEPTH, 2, D_MODEL, 2 * D_FF), D_MODEL ** -0.5)
    ffn_w_out = nrm(ks[10], (DEPTH, 2, D_FF, D_MODEL), D_FF ** -0.5)
    w_in = nrm(ks[11], (DEPTH, D_MODEL, IN_COLS), D_MODEL ** -0.5)
    hg_lower_bounds = nrm(ks[12], (N_DIRS, DEPTH, HG_QK_DIM), 0.5)
    hg_norm_w = 1.0 + nrm(ks[13], (DEPTH, HG_DV), 0.01)
    gd_conv_w = nrm(ks[14], (DEPTH, CONV_K, GD_CONV_DIM), CONV_K ** -0.5)
    gd_A_log = jnp.log(jax.random.uniform(ks[15], (DEPTH, N_DIRS, GD_HEADS), f32, 1.0, 16.0))
    dt0 = jnp.exp(jax.random.uniform(ks[16], (DEPTH, N_DIRS, GD_HEADS), f32,
                                     math.log(1e-3), math.log(1e-1)))
    gd_dt_bias = dt0 + jnp.log(-jnp.expm1(-dt0))
    gd_norm_w = 1.0 + nrm(ks[17], (DEPTH, GD_DV), 0.01)
    w_out = nrm(ks[18], (DEPTH, MIX_WIDTH, D_MODEL), MIX_WIDTH ** -0.5)
    final_norm_w = 1.0 + nrm(ks[19], (D_MODEL,), 0.01)
    return {'x_prompt': x_prompt, 'x_sample': x_sample, 'c': c,
            'state_hgrn2': state_hgrn2, 'state_gdn': state_gdn, 'c_ctx': c_ctx,
            'norm_w': norm_w, 'w_mod': w_mod, 'b_mod': b_mod,
            'ffn_w_in': ffn_w_in, 'ffn_w_out': ffn_w_out, 'w_in': w_in,
            'hg_lower_bounds': hg_lower_bounds, 'hg_norm_w': hg_norm_w,
            'gd_conv_w': gd_conv_w, 'gd_A_log': gd_A_log, 'gd_dt_bias': gd_dt_bias,
            'gd_norm_w': gd_norm_w, 'w_out': w_out, 'final_norm_w': final_norm_w}


def reference(x_prompt, x_sample, c, state_hgrn2, state_gdn, c_ctx, norm_w, w_mod, b_mod,
              ffn_w_in, ffn_w_out, w_in, hg_lower_bounds, hg_norm_w, gd_conv_w,
              gd_A_log, gd_dt_bias, gd_norm_w, w_out, final_norm_w):
    p = {'norm_w': norm_w, 'ffn_w_in': ffn_w_in, 'ffn_w_out': ffn_w_out, 'w_in': w_in,
         'hg_norm_w': hg_norm_w, 'gd_conv_w': gd_conv_w, 'gd_A_log': gd_A_log,
         'gd_dt_bias': gd_dt_bias, 'gd_norm_w': gd_norm_w, 'w_out': w_out}
    lbs = _hgrn2_lower_bounds(hg_lower_bounds)
    n_ctx = x_prompt.shape[0]
    zero_hg = jnp.zeros((n_ctx, N_DIRS, HG_HEADS, HG_DK, HG_DV), jnp.float32)
    zero_gd = jnp.zeros((n_ctx, N_DIRS, GD_HEADS, GD_DK, GD_DV), jnp.float32)
    x_p, x_s = x_prompt, x_sample
    new_hg, new_gd = [], []
    for l in range(DEPTH):
        mod_ctx = _modulation(c_ctx[None, :], w_mod[l], b_mod[l])
        x_p, s_hg, s_gd = _layer(x_p, mod_ctx, p, lbs, l, zero_hg, zero_gd, False)
        new_hg.append(s_hg)
        new_gd.append(s_gd)
        mod_lat = _modulation(c, w_mod[l], b_mod[l])
        x_s, _, _ = _layer(x_s, mod_lat, p, lbs, l, state_hgrn2[:, l], state_gdn[:, l], True)
    y_prompt = _rms_norm(x_p, final_norm_w)
    y_sample = _rms_norm(x_s, final_norm_w)
    new_state_hgrn2 = jnp.stack(new_hg, axis=1)
    new_state_gdn = jnp.stack(new_gd, axis=1)
    return (y_prompt, y_sample, new_state_hgrn2, new_state_gdn)
```

```python
import functools

import jax
import jax.numpy as jnp
from jax import lax
from jax.experimental import pallas as pl
from jax.experimental.pallas import tpu as pltpu

F32 = jnp.float32
BF16 = jnp.bfloat16
HIGHEST = lax.Precision.HIGHEST

D_MODEL = 2048
BATCH = 32
SEQ = 256
DEPTH = 2
DEC_BATCH = 8
DEC_SEQ = 2048
GRID_W = 64
HEADS = 8
HEAD_DIM = 128
GROUP_W = HEADS * HEAD_DIM
CONV_K = 5
D_FF = 5504
N_MOD = 9
EPS = 1e-6

N_PROMPT = BATCH * SEQ
N_SAMPLE = DEC_BATCH * DEC_SEQ
N_TOK = N_PROMPT + N_SAMPLE
N_COND = 16
N_MAIN = 9 * GROUP_W
TAIL_W = 128

V7X_LANES = 128
FF_TILE = 512
D_FF_PAD = 5632
FFN_TM = 512
IN_TM = 1024
IN_TN = 512
OUT_TM = 256
NORM_TM = 1024
MOD_TN = 1024
SCAN_TB = 256
HG_C = 32
GD_C = 64
EXP_CLAMP = 80.0
VMEM_LIMIT = 52 * 1024 * 1024


def _mod_row(i, tm):
    n_ctx = N_PROMPT // tm
    return jnp.where(i < n_ctx, 0, 1 + (i - n_ctx) // (DEC_SEQ // tm))


def _sigmoid(x):
    return jax.nn.sigmoid(x)


def _softplus(x):
    return jnp.maximum(x, 0.0) + jnp.log1p(jnp.exp(-jnp.abs(x)))


def _dot(a, b):
    return jnp.dot(a, b, preferred_element_type=F32)


def _dot_nt(a, b):
    return lax.dot_general(a, b, (((1,), (1,)), ((), ())), preferred_element_type=F32)


def _dot_tn(a, b):
    return lax.dot_general(a, b, (((0,), (0,)), ((), ())), preferred_element_type=F32)


def _split_bf16(x):
    hi = x.astype(BF16)
    return hi, (x - hi.astype(F32)).astype(BF16)


def _dot3(a, b):
    ah, al = _split_bf16(a)
    bh, bl = _split_bf16(b)
    return _dot(ah, bh) + (_dot(ah, bl) + _dot(al, bh))


def _norm_modulate(x, nw, scale, shift):
    ms = jnp.mean(x * x, axis=-1, keepdims=True)
    return x * lax.rsqrt(ms + EPS) * nw * (1.0 + scale) + shift


def _mod_kernel(c_ref, w_ref, b_ref, o_ref):
    c = c_ref[...]
    s = (c * _sigmoid(c)).astype(BF16)
    o_ref[0] = _dot(s, w_ref[0].astype(BF16)) + b_ref[0]


def _modulation(cond, w_mod, b_mod):
    n_out = N_MOD * D_MODEL
    out = pl.pallas_call(
        _mod_kernel,
        grid=(DEPTH, n_out // MOD_TN),
        in_specs=[
            pl.BlockSpec((N_COND, D_MODEL), lambda l, j: (0, 0)),
            pl.BlockSpec((1, D_MODEL, MOD_TN), lambda l, j: (l, 0, j)),
            pl.BlockSpec((1, 1, MOD_TN), lambda l, j: (l, 0, j)),
        ],
        out_specs=pl.BlockSpec((1, N_COND, MOD_TN), lambda l, j: (l, 0, j)),
        out_shape=jax.ShapeDtypeStruct((DEPTH, N_COND, n_out), F32),
        compiler_params=pltpu.CompilerParams(
            dimension_semantics=("arbitrary", "arbitrary"), vmem_limit_bytes=VMEM_LIMIT),
    )(cond, w_mod, b_mod.reshape(DEPTH, 1, n_out))
    return out.reshape(DEPTH, N_COND, N_MOD, D_MODEL)


def _ffn_kernel(x_ref, mod_ref, nw_ref, wg_ref, wu_ref, w2_ref, o_ref, hn_ref, acc_ref, *, i_mod, nj):
    j = pl.program_id(1)

    @pl.when(j == 0)
    def _():
        h = _norm_modulate(x_ref[...], nw_ref[...],
                           mod_ref[0, i_mod + 1:i_mod + 2, :], mod_ref[0, i_mod:i_mod + 1, :])
        hn_ref[...] = h.astype(BF16)
        acc_ref[...] = jnp.zeros_like(acc_ref)

    h = hn_ref[...]
    g = _dot(h, wg_ref[...])
    u = _dot(h, wu_ref[...])
    a = (g * _sigmoid(g) * u).astype(BF16)
    acc_ref[...] += _dot(a, w2_ref[...])

    @pl.when(j == nj - 1)
    def _():
        o_ref[...] = x_ref[...] + 0.5 * mod_ref[0, i_mod + 2:i_mod + 3, :] * acc_ref[...]


def _ffn(x, mod, nw, w1, w2, i_mod):
    nj = D_FF_PAD // FF_TILE
    return pl.pallas_call(
        functools.partial(_ffn_kernel, i_mod=i_mod, nj=nj),
        grid=(N_TOK // FFN_TM, nj),
        in_specs=[
            pl.BlockSpec((FFN_TM, D_MODEL), lambda i, j: (i, 0)),
            pl.BlockSpec((1, N_MOD, D_MODEL), lambda i, j: (_mod_row(i, FFN_TM), 0, 0)),
            pl.BlockSpec((1, D_MODEL), lambda i, j: (0, 0)),
            pl.BlockSpec((D_MODEL, FF_TILE), lambda i, j: (0, j)),
            pl.BlockSpec((D_MODEL, FF_TILE), lambda i, j: (0, nj + j)),
            pl.BlockSpec((FF_TILE, D_MODEL), lambda i, j: (j, 0)),
        ],
        out_specs=pl.BlockSpec((FFN_TM, D_MODEL), lambda i, j: (i, 0)),
        out_shape=jax.ShapeDtypeStruct((N_TOK, D_MODEL), F32),
        scratch_shapes=[pltpu.VMEM((FFN_TM, D_MODEL), BF16), pltpu.VMEM((FFN_TM, D_MODEL), F32)],
        compiler_params=pltpu.CompilerParams(
            dimension_semantics=("parallel", "arbitrary"), vmem_limit_bytes=VMEM_LIMIT),
    )(x, mod, nw.reshape(1, D_MODEL), w1, w1, w2)


def _mixer_in_kernel(x_ref, mod_ref, nw_ref, w_ref, wt_ref, o_ref, t_ref, hn_ref):
    j = pl.program_id(1)

    @pl.when(j == 0)
    def _():
        h = _norm_modulate(x_ref[...], nw_ref[...], mod_ref[0, 4:5, :], mod_ref[0, 3:4, :]).astype(BF16)
        hn_ref[...] = h
        t_ref[...] = _dot(h, wt_ref[...])

    o_ref[...] = _dot(hn_ref[...], w_ref[...])


def _mixer_in(x, mod, nw, w_main, w_tail):
    return pl.pallas_call(
        _mixer_in_kernel,
        grid=(N_TOK // IN_TM, N_MAIN // IN_TN),
        in_specs=[
            pl.BlockSpec((IN_TM, D_MODEL), lambda i, j: (i, 0)),
            pl.BlockSpec((1, N_MOD, D_MODEL), lambda i, j: (_mod_row(i, IN_TM), 0, 0)),
            pl.BlockSpec((1, D_MODEL), lambda i, j: (0, 0)),
            pl.BlockSpec((D_MODEL, IN_TN), lambda i, j: (0, j)),
            pl.BlockSpec((D_MODEL, TAIL_W), lambda i, j: (0, 0)),
        ],
        out_specs=[
            pl.BlockSpec((IN_TM, IN_TN), lambda i, j: (i, j)),
            pl.BlockSpec((IN_TM, TAIL_W), lambda i, j: (i, 0)),
        ],
        out_shape=[jax.ShapeDtypeStruct((N_TOK, N_MAIN), F32),
                   jax.ShapeDtypeStruct((N_TOK, TAIL_W), F32)],
        scratch_shapes=[pltpu.VMEM((IN_TM, D_MODEL), BF16)],
        compiler_params=pltpu.CompilerParams(
            dimension_semantics=("parallel", "arbitrary"), vmem_limit_bytes=VMEM_LIMIT),
    )(x, mod, nw.reshape(1, D_MODEL), w_main, w_tail)


def _tri_mask(n, reverse):
    row = lax.broadcasted_iota(jnp.int32, (n, n), 0)
    col = lax.broadcasted_iota(jnp.int32, (n, n), 1)
    return (row <= col) if reverse else (row >= col)


def _hgrn2_kernel(*refs, reverse, has_s0, want_fin, nblk):
    q_ref, v_ref, f_ref, lb_ref, llb_ref = refs[:5]
    pos = 5
    s0_ref = None
    if has_s0:
        s0_ref = refs[pos]
        pos += 1
    o_ref = refs[pos]
    pos += 1
    fin_ref = None
    if want_fin:
        fin_ref = refs[pos]
        pos += 1
    st_ref = refs[pos]

    blk = pl.program_id(1)
    c = HG_C
    nch = SCAN_TB // c

    @pl.when(blk == 0)
    def _():
        for h in range(HEADS):
            if has_s0:
                st_ref[h] = s0_ref[0, 0, 0, h].T
            else:
                st_ref[h] = jnp.zeros((HEAD_DIM, HEAD_DIM), F32)

    incl = _tri_mask(c, reverse)
    tri = incl.astype(F32)
    mid = c // 2

    def chunk(n, carry):
        nn = (nch - 1 - n) if reverse else n
        r0 = pl.multiple_of(nn * c, c)
        rows = pl.ds(r0, c)
        for h in range(HEADS):
            hc = slice(h * HEAD_DIM, (h + 1) * HEAD_DIM)
            fr = f_ref[rows, hc]
            qr = q_ref[rows, hc]
            vb = v_ref[rows, hc].astype(BF16)
            lb = lb_ref[:, hc]
            llb = llb_ref[:, hc]
            b = jnp.log1p(-lb) + jnp.minimum(fr, 0.0) - jnp.log1p(jnp.exp(-jnp.abs(fr)))
            m = jnp.maximum(llb, b)
            logf = m + jnp.log(jnp.exp(llb - m) + jnp.exp(b - m))
            k = (1.0 - lb) * _sigmoid(-fr)
            q = qr * _sigmoid(qr) * (HEAD_DIM ** -0.5)
            bc = jnp.dot(tri, logf, precision=HIGHEST, preferred_element_type=F32)
            btot = jnp.sum(logf, axis=0, keepdims=True)
            r = bc[mid:mid + 1, :]
            q0 = (q * jnp.exp(bc)).astype(BF16)
            qm = (q * jnp.exp(jnp.minimum(bc - r, EXP_CLAMP))).astype(BF16)
            km = (k * jnp.exp(jnp.minimum(r - bc, EXP_CLAMP))).astype(BF16)
            kl = (k * jnp.exp(btot - bc)).astype(BF16)
            att = jnp.where(incl, _dot_nt(qm, km), 0.0).astype(BF16)
            st = st_ref[h]
            o_ref[rows, hc] = _dot(att, vb) + _dot_nt(q0, st.astype(BF16))
            st_ref[h] = st * jnp.exp(btot) + _dot_tn(vb, kl)
        return carry

    lax.fori_loop(0, nch, chunk, 0)

    if want_fin:
        @pl.when(blk == nblk - 1)
        def _():
            for h in range(HEADS):
                fin_ref[0, h] = st_ref[h].T


def _hgrn2_scan(proj, lb, llb, state, *, layer, reverse, sample):
    d = 1 if reverse else 0
    nseq, nblk = (DEC_BATCH, DEC_SEQ // SCAN_TB) if sample else (BATCH, SEQ // SCAN_TB)
    row0 = (N_PROMPT // SCAN_TB) if sample else 0

    def tok_blk(s, b):
        return s * nblk + ((nblk - 1 - b) if reverse else b)

    in_specs = [
        pl.BlockSpec((SCAN_TB, GROUP_W), lambda s, b: (row0 + tok_blk(s, b), 0)),
        pl.BlockSpec((SCAN_TB, GROUP_W), lambda s, b: (row0 + tok_blk(s, b), 1)),
        pl.BlockSpec((SCAN_TB, GROUP_W), lambda s, b: (row0 + tok_blk(s, b), 2 + d)),
        pl.BlockSpec((1, GROUP_W), lambda s, b: (0, 0)),
        pl.BlockSpec((1, GROUP_W), lambda s, b: (0, 0)),
    ]
    args = [proj, proj, proj, lb, llb]
    out_specs = [pl.BlockSpec((SCAN_TB, GROUP_W), lambda s, b: (tok_blk(s, b), 0))]
    out_shape = [jax.ShapeDtypeStruct((nseq * nblk * SCAN_TB, GROUP_W), F32)]
    if sample:
        in_specs.append(pl.BlockSpec((1, 1, 1, HEADS, HEAD_DIM, HEAD_DIM),
                                     lambda s, b: (s, layer, d, 0, 0, 0)))
        args.append(state)
    else:
        out_specs.append(pl.BlockSpec((1, HEADS, HEAD_DIM, HEAD_DIM), lambda s, b: (s, 0, 0, 0)))
        out_shape.append(jax.ShapeDtypeStruct((nseq, HEADS, HEAD_DIM, HEAD_DIM), F32))
    return pl.pallas_call(
        functools.partial(_hgrn2_kernel, reverse=reverse, has_s0=sample, want_fin=not sample, nblk=nblk),
        grid=(nseq, nblk),
        in_specs=in_specs,
        out_specs=out_specs,
        out_shape=out_shape,
        scratch_shapes=[pltpu.VMEM((HEADS, HEAD_DIM, HEAD_DIM), F32)],
        compiler_params=pltpu.CompilerParams(
            dimension_semantics=("parallel", "arbitrary"), vmem_limit_bytes=VMEM_LIMIT),
    )(*args)


def _gdn_pre_kernel(q_ref, k_ref, v_ref, cwq_ref, cwk_ref, cwv_ref, qo_ref, ko_ref, vo_ref):
    i = pl.program_id(0)
    period = jnp.where(i < N_PROMPT // SCAN_TB, SCAN_TB, GRID_W)
    pos = lax.broadcasted_iota(jnp.int32, (SCAN_TB, GROUP_W), 0) & (period - 1)

    def conv_silu(x_ref, cw_ref):
        x = x_ref[...]
        acc = x * cw_ref[CONV_K // 2:CONV_K // 2 + 1, :]
        for j in range(CONV_K):
            off = j - CONV_K // 2
            if off == 0:
                continue
            xs = pltpu.roll(x, (-off) % SCAN_TB, 0)
            valid = (pos + off >= 0) & (pos + off < period)
            acc = acc + jnp.where(valid, xs, 0.0) * cw_ref[j:j + 1, :]
        return acc * _sigmoid(acc)

    def l2norm_heads(y, o_ref, scale):
        for h in range(HEADS):
            hc = slice(h * HEAD_DIM, (h + 1) * HEAD_DIM)
            yh = y[:, hc]
            o_ref[:, hc] = yh * (lax.rsqrt(jnp.sum(yh * yh, axis=-1, keepdims=True) + EPS) * scale)

    l2norm_heads(conv_silu(q_ref, cwq_ref), qo_ref, HEAD_DIM ** -0.5)
    l2norm_heads(conv_silu(k_ref, cwk_ref), ko_ref, 1.0)
    vo_ref[...] = conv_silu(v_ref, cwv_ref)


def _gdn_pre(proj, conv_w):
    cw = [conv_w[:, i * GROUP_W:(i + 1) * GROUP_W] for i in range(3)]
    tok = lambda col: pl.BlockSpec((SCAN_TB, GROUP_W), lambda i: (i, col))
    cws = pl.BlockSpec((CONV_K, GROUP_W), lambda i: (0, 0))
    out = jax.ShapeDtypeStruct((N_TOK, GROUP_W), F32)
    return pl.pallas_call(
        _gdn_pre_kernel,
        grid=(N_TOK // SCAN_TB,),
        in_specs=[tok(5), tok(6), tok(7), cws, cws, cws],
        out_specs=[tok(0), tok(0), tok(0)],
        out_shape=[out, out, out],
        compiler_params=pltpu.CompilerParams(
            dimension_semantics=("parallel",), vmem_limit_bytes=VMEM_LIMIT),
    )(proj, proj, proj, *cw)


def _gdn_kernel(*refs, reverse, has_s0, want_fin, nblk, d):
    q_ref, k_ref, v_ref, ab_ref, abt_ref, arow_ref, drow_ref, acol_ref, dcol_ref = refs[:9]
    pos = 9
    s0_ref = None
    if has_s0:
        s0_ref = refs[pos]
        pos += 1
    o_ref = refs[pos]
    pos += 1
    fin_ref = None
    if want_fin:
        fin_ref = refs[pos]
        pos += 1
    s_ref = refs[pos]

    blk = pl.program_id(1)
    c = GD_C

    @pl.when(blk == 0)
    def _():
        for h in range(HEADS):
            if has_s0:
                s_ref[h] = s0_ref[0, 0, 0, h]
            else:
                s_ref[h] = jnp.zeros((HEAD_DIM, HEAD_DIM), F32)

    incl = _tri_mask(c, reverse)
    incl_t = _tri_mask(c, not reverse)
    row = lax.broadcasted_iota(jnp.int32, (c, c), 0)
    col = lax.broadcasted_iota(jnp.int32, (c, c), 1)
    strict = incl & (row != col)
    eye = (row == col).astype(F32)
    same_blk = {lvl: (row >> lvl) == (col >> lvl) for lvl in range(3, c.bit_length())}

    ab = ab_ref[...]
    g_tok = -jnp.exp(arow_ref[...]) * _softplus(ab + drow_ref[...])
    beta_tok = _sigmoid(ab)
    gcum = jnp.dot(incl.astype(F32), g_tok, precision=HIGHEST, preferred_element_type=F32)
    g_head = -jnp.exp(acol_ref[...]) * _softplus(abt_ref[0] + dcol_ref[...])
    gcum_t = jnp.dot(g_head, incl_t.astype(F32), precision=HIGHEST, preferred_element_type=F32)
    last = 0 if reverse else c - 1

    for h in range(HEADS):
        hc = slice(h * HEAD_DIM, (h + 1) * HEAD_DIM)
        ia = d * HEADS + h
        ib = 2 * HEADS + d * HEADS + h
        qn = q_ref[:, hc]
        kn = k_ref[:, hc]
        vv = v_ref[:, hc]
        gcol = gcum[:, ia:ia + 1]
        grow = gcum_t[ia:ia + 1, :]
        bcol = beta_tok[:, ib:ib + 1]
        glast = gcum[last:last + 1, ia:ia + 1]
        decay = jnp.exp(jnp.where(incl, gcol - grow, -1e30))
        knb = kn.astype(BF16)
        kb = kn * bcol
        a = jnp.where(strict, _dot_nt(kb.astype(BF16), knb) * decay, 0.0)
        p = jnp.where(same_blk[3], -a, 0.0)
        t = eye + p
        p = _dot3(p, p)
        t = t + _dot3(t, p)
        p = _dot3(p, p)
        t = t + _dot3(t, p)
        for lvl in range(3, GD_C.bit_length() - 1):
            n = jnp.where(same_blk[lvl + 1] & jnp.logical_not(same_blk[lvl]), a, 0.0)
            t = t - _dot3(_dot3(t, n), t)
        rhs = jnp.concatenate([vv * bcol, kb * jnp.exp(gcol)], axis=-1)
        sol = _dot3(t, rhs)
        u = sol[:, :HEAD_DIM]
        w = sol[:, HEAD_DIM:]
        aqk = (_dot_nt(qn.astype(BF16), knb) * decay).astype(BF16)
        s = s_ref[h]
        sb = s.astype(BF16)
        v_new = u - _dot(w.astype(BF16), sb)
        vnb = v_new.astype(BF16)
        o_ref[:, hc] = _dot((qn * jnp.exp(gcol)).astype(BF16), sb) + _dot(aqk, vnb)
        kdec = (kn * jnp.exp(glast - gcol)).astype(BF16)
        s_ref[h] = s * jnp.exp(glast) + _dot_tn(kdec, vnb)

    if want_fin:
        @pl.when(blk == nblk - 1)
        def _():
            for h in range(HEADS):
                fin_ref[0, h] = s_ref[h]


def _gdn_scan(qn, kn, vc, tail, tail_t, arow, drow, acol, dcol, state, *, layer, reverse, sample):
    d = 1 if reverse else 0
    c = GD_C
    nseq, nblk = (DEC_BATCH, DEC_SEQ // c) if sample else (BATCH, SEQ // c)
    row0 = (N_PROMPT // c) if sample else 0

    def tok_blk(s, b):
        return s * nblk + ((nblk - 1 - b) if reverse else b)

    tok = pl.BlockSpec((c, GROUP_W), lambda s, b: (row0 + tok_blk(s, b), 0))
    vec_row = pl.BlockSpec((1, TAIL_W), lambda s, b: (0, 0))
    vec_col = pl.BlockSpec((TAIL_W, 1), lambda s, b: (0, 0))
    in_specs = [
        tok, tok, tok,
        pl.BlockSpec((c, TAIL_W), lambda s, b: (row0 + tok_blk(s, b), 0)),
        pl.BlockSpec((1, TAIL_W, c), lambda s, b: (row0 + tok_blk(s, b), 0, 0)),
        vec_row, vec_row, vec_col, vec_col,
    ]
    args = [qn, kn, vc, tail, tail_t, arow, drow, acol, dcol]
    out_specs = [pl.BlockSpec((c, GROUP_W), lambda s, b: (tok_blk(s, b), 0))]
    out_shape = [jax.ShapeDtypeStruct((nseq * nblk * c, GROUP_W), F32)]
    if sample:
        in_specs.append(pl.BlockSpec((1, 1, 1, HEADS, HEAD_DIM, HEAD_DIM),
                                     lambda s, b: (s, layer, d, 0, 0, 0)))
        args.append(state)
    else:
        out_specs.append(pl.BlockSpec((1, HEADS, HEAD_DIM, HEAD_DIM), lambda s, b: (s, 0, 0, 0)))
        out_shape.append(jax.ShapeDtypeStruct((nseq, HEADS, HEAD_DIM, HEAD_DIM), F32))
    return pl.pallas_call(
        functools.partial(_gdn_kernel, reverse=reverse, has_s0=sample, want_fin=not sample, nblk=nblk, d=d),
        grid=(nseq, nblk),
        in_specs=in_specs,
        out_specs=out_specs,
        out_shape=out_shape,
        scratch_shapes=[pltpu.VMEM((HEADS, HEAD_DIM, HEAD_DIM), F32)],
        compiler_params=pltpu.CompilerParams(
            dimension_semantics=("parallel", "arbitrary"), vmem_limit_bytes=VMEM_LIMIT),
    )(*args)


def _mixer_out_kernel(x_ref, mod_ref, ohf_ref, ohb_ref, ogf_ref, ogb_ref, hgate_ref, ggate_ref,
                      hw_ref, gw_ref, wo_ref, o_ref, cat_ref):
    def norm_gate(of_ref, ob_ref, gate_ref, w_ref, base):
        for h in range(HEADS):
            hc = slice(h * HEAD_DIM, (h + 1) * HEAD_DIM)
            o = of_ref[:, hc] + ob_ref[:, hc]
            on = o * lax.rsqrt(jnp.mean(o * o, axis=-1, keepdims=True) + EPS) * w_ref[...]
            gate = gate_ref[:, hc]
            cat_ref[:, base + h * HEAD_DIM:base + (h + 1) * HEAD_DIM] = (on * (gate * _sigmoid(gate))).astype(BF16)

    norm_gate(ohf_ref, ohb_ref, hgate_ref, hw_ref, 0)
    norm_gate(ogf_ref, ogb_ref, ggate_ref, gw_ref, GROUP_W)
    o_ref[...] = x_ref[...] + mod_ref[0, 5:6, :] * _dot(cat_ref[...], wo_ref[...])


def _mixer_out(x, mod, o_hg_f, o_hg_b, o_gd_f, o_gd_b, proj, hg_norm_w, gd_norm_w, w_out):
    tok = lambda col: pl.BlockSpec((OUT_TM, GROUP_W), lambda i: (i, col))
    vec = pl.BlockSpec((1, HEAD_DIM), lambda i: (0, 0))
    return pl.pallas_call(
        _mixer_out_kernel,
        grid=(N_TOK // OUT_TM,),
        in_specs=[
            pl.BlockSpec((OUT_TM, D_MODEL), lambda i: (i, 0)),
            pl.BlockSpec((1, N_MOD, D_MODEL), lambda i: (_mod_row(i, OUT_TM), 0, 0)),
            tok(0), tok(0), tok(0), tok(0), tok(4), tok(8),
            vec, vec,
            pl.BlockSpec((D_MODEL, D_MODEL), lambda i: (0, 0)),
        ],
        out_specs=pl.BlockSpec((OUT_TM, D_MODEL), lambda i: (i, 0)),
        out_shape=jax.ShapeDtypeStruct((N_TOK, D_MODEL), F32),
        scratch_shapes=[pltpu.VMEM((OUT_TM, D_MODEL), BF16)],
        compiler_params=pltpu.CompilerParams(
            dimension_semantics=("parallel",), vmem_limit_bytes=VMEM_LIMIT),
    )(x, mod, o_hg_f, o_hg_b, o_gd_f, o_gd_b, proj, proj,
      hg_norm_w.reshape(1, HEAD_DIM), gd_norm_w.reshape(1, HEAD_DIM), w_out)


def _final_norm_kernel(x_ref, w_ref, o_ref):
    x = x_ref[...]
    o_ref[...] = x * lax.rsqrt(jnp.mean(x * x, axis=-1, keepdims=True) + EPS) * w_ref[...]


def _final_norm(x, w, row0, n_rows):
    blk0 = row0 // NORM_TM
    return pl.pallas_call(
        _final_norm_kernel,
        grid=(n_rows // NORM_TM,),
        in_specs=[pl.BlockSpec((NORM_TM, D_MODEL), lambda i: (blk0 + i, 0)),
                  pl.BlockSpec((1, D_MODEL), lambda i: (0, 0))],
        out_specs=pl.BlockSpec((NORM_TM, D_MODEL), lambda i: (i, 0)),
        out_shape=jax.ShapeDtypeStruct((n_rows, D_MODEL), F32),
        compiler_params=pltpu.CompilerParams(
            dimension_semantics=("parallel",), vmem_limit_bytes=VMEM_LIMIT),
    )(x, w.reshape(1, D_MODEL))


def _pad_cols(w, n):
    return jnp.pad(w, ((0, 0), (0, n - w.shape[1])))


def _lane_vec(v):
    return jnp.pad(v.reshape(-1).astype(F32), (0, TAIL_W - 2 * HEADS))


def kernel(x_prompt, x_sample, c, state_hgrn2, state_gdn, c_ctx, norm_w, w_mod, b_mod, ffn_w_in, ffn_w_out,
           w_in, hg_lower_bounds, hg_norm_w, gd_conv_w, gd_A_log, gd_dt_bias, gd_norm_w, w_out, final_norm_w):
    x = jnp.concatenate([x_prompt.reshape(N_PROMPT, D_MODEL), x_sample.reshape(N_SAMPLE, D_MODEL)], axis=0)
    cond = jnp.concatenate([c_ctx[None, :], c, jnp.zeros((N_COND - 1 - DEC_BATCH, D_MODEL), F32)], axis=0)
    mods = _modulation(cond, w_mod, b_mod)

    cs = jnp.cumsum(jax.nn.softmax(hg_lower_bounds.astype(F32), axis=1), axis=1)
    lbs = cs - cs[:, :1]

    new_hg, new_gd = [], []
    for l in range(DEPTH):
        mod = mods[l]
        for half in range(2):
            if half == 1:
                w_main = w_in[l][:, :N_MAIN].astype(BF16)
                w_tail = _pad_cols(w_in[l][:, N_MAIN:], TAIL_W).astype(BF16)
                proj, tail = _mixer_in(x, mod, norm_w[l, 1], w_main, w_tail)
                tail_t = tail.reshape(N_TOK // GD_C, GD_C, TAIL_W).transpose(0, 2, 1)
                qn, kn, vc = _gdn_pre(proj, gd_conv_w[l])
                arow = _lane_vec(gd_A_log[l])
                drow = _lane_vec(gd_dt_bias[l])
                o_hg, o_gd, fin_hg, fin_gd = [], [], [], []
                for d in range(2):
                    lb = lbs[d, l].reshape(1, GROUP_W)
                    llb = jnp.log(lb)
                    op, fin = _hgrn2_scan(proj, lb, llb, None, layer=l, reverse=d == 1, sample=False)
                    (os_,) = _hgrn2_scan(proj, lb, llb, state_hgrn2, layer=l, reverse=d == 1, sample=True)
                    o_hg.append(jnp.concatenate([op, os_], axis=0))
                    fin_hg.append(fin)
                    gd_args = (qn, kn, vc, tail, tail_t, arow.reshape(1, TAIL_W), drow.reshape(1, TAIL_W),
                               arow.reshape(TAIL_W, 1), drow.reshape(TAIL_W, 1))
                    op, fin = _gdn_scan(*gd_args, None, layer=l, reverse=d == 1, sample=False)
                    (os_,) = _gdn_scan(*gd_args, state_gdn, layer=l, reverse=d == 1, sample=True)
                    o_gd.append(jnp.concatenate([op, os_], axis=0))
                    fin_gd.append(fin)
                new_hg.append(jnp.stack(fin_hg, axis=1))
                new_gd.append(jnp.stack(fin_gd, axis=1))
                x = _mixer_out(x, mod, o_hg[0], o_hg[1], o_gd[0], o_gd[1], proj,
                               hg_norm_w[l], gd_norm_w[l], w_out[l].astype(BF16))
            k = half
            if half == 1:
                pass
            w1 = jnp.concatenate([_pad_cols(ffn_w_in[l, k][:, :D_FF], D_FF_PAD),
                                  _pad_cols(ffn_w_in[l, k][:, D_FF:], D_FF_PAD)], axis=1).astype(BF16)
            w2 = jnp.pad(ffn_w_out[l, k], ((0, D_FF_PAD - D_FF), (0, 0))).astype(BF16)
            x = _ffn(x, mod, norm_w[l, 2 * k], w1, w2, 6 * k)

    y_prompt = _final_norm(x, final_norm_w, 0, N_PROMPT).reshape(BATCH, SEQ, D_MODEL)
    y_sample = _final_norm(x, final_norm_w, N_PROMPT, N_SAMPLE).reshape(DEC_BATCH, DEC_SEQ, D_MODEL)
    return (y_prompt, y_sample, jnp.stack(new_hg, axis=1), jnp.stack(new_gd, axis=1))
```

```python
import functools

import jax
import jax.numpy as jnp
from jax import lax
from jax.experimental import pallas as pl
from jax.experimental.pallas import tpu as pltpu

F32 = jnp.float32
BF16 = jnp.bfloat16

D_MODEL = 2048
BATCH = 32
SEQ = 256
DEPTH = 2
DEC_BATCH = 8
DEC_SEQ = 2048
GRID_W = 64
HEADS = 8
HEAD_DIM = 128
GROUP_W = HEADS * HEAD_DIM
CONV_K = 5
D_FF = 5504
N_MOD = 9
EPS = 1e-6

N_PROMPT = BATCH * SEQ
N_SAMPLE = DEC_BATCH * DEC_SEQ
N_TOK = N_PROMPT + N_SAMPLE
N_COND = 16
N_MAIN = 9 * GROUP_W
TAIL_W = 128

FF_TILE = 512
D_FF_PAD = 5632
FFN_TM = 512
IN_TM = 1024
IN_TN = 1024
OUT_TM = 256
NORM_TM = 1024
MOD_TN = 1024
SCAN_TB = 256
HG_C = 32
GD_C = 64
EXP_CLAMP = 80.0
VMEM_LIMIT = 52 * 1024 * 1024


def _mod_row(i, tm):
    n_ctx = N_PROMPT // tm
    return jnp.where(i < n_ctx, 0, 1 + (i - n_ctx) // (DEC_SEQ // tm))


def _sigmoid(x):
    return jax.nn.sigmoid(x)


def _softplus(x):
    return jnp.maximum(x, 0.0) + jnp.log1p(jnp.exp(-jnp.abs(x)))


def _dot(a, b):
    return jnp.dot(a, b, preferred_element_type=F32)


def _dot_nt(a, b):
    return lax.dot_general(a, b, (((1,), (1,)), ((), ())), preferred_element_type=F32)


def _dot_tn(a, b):
    return lax.dot_general(a, b, (((0,), (0,)), ((), ())), preferred_element_type=F32)


def _split3_bf16(x):
    hi = x.astype(BF16)
    r1 = x - hi.astype(F32)
    mid = r1.astype(BF16)
    return hi, mid, (r1 - mid.astype(F32)).astype(BF16)


def _dot_exact_lhs(a, x):
    hi, mid, lo = _split3_bf16(x)
    return _dot(a, hi) + (_dot(a, mid) + _dot(a, lo))


def _dot_exact_rhs(x, a):
    hi, mid, lo = _split3_bf16(x)
    return _dot(hi, a) + (_dot(mid, a) + _dot(lo, a))


def _each(f, *lists):
    return [f(*args) for args in zip(*lists)]


def _norm_modulate(x, nw, scale, shift):
    ms = jnp.mean(x * x, axis=-1, keepdims=True)
    return x * lax.rsqrt(ms + EPS) * nw * (1.0 + scale) + shift


def _mod_kernel(c_ref, w_ref, b_ref, o_ref):
    c = c_ref[...]
    s = (c * _sigmoid(c)).astype(BF16)
    o_ref[0] = _dot(s, w_ref[0].astype(BF16)) + b_ref[0]


def _modulation(cond, w_mod, b_mod):
    n_out = N_MOD * D_MODEL
    out = pl.pallas_call(
        _mod_kernel,
        grid=(DEPTH, n_out // MOD_TN),
        in_specs=[
            pl.BlockSpec((N_COND, D_MODEL), lambda l, j: (0, 0)),
            pl.BlockSpec((1, D_MODEL, MOD_TN), lambda l, j: (l, 0, j)),
            pl.BlockSpec((1, 1, MOD_TN), lambda l, j: (l, 0, j)),
        ],
        out_specs=pl.BlockSpec((1, N_COND, MOD_TN), lambda l, j: (l, 0, j)),
        out_shape=jax.ShapeDtypeStruct((DEPTH, N_COND, n_out), F32),
        name="modulation",
        compiler_params=pltpu.CompilerParams(
            dimension_semantics=("arbitrary", "arbitrary"), vmem_limit_bytes=VMEM_LIMIT),
    )(cond, w_mod, b_mod.reshape(DEPTH, 1, n_out))
    return out.reshape(DEPTH, N_COND, N_MOD, D_MODEL)


def _ffn_kernel(x_ref, mod_ref, nw_ref, wg_ref, wu_ref, w2_ref, o_ref, hn_ref, acc_ref, *, i_mod, nj):
    j = pl.program_id(1)

    @pl.when(j == 0)
    def _():
        h = _norm_modulate(x_ref[...], nw_ref[...],
                           mod_ref[0, i_mod + 1:i_mod + 2, :], mod_ref[0, i_mod:i_mod + 1, :])
        hn_ref[...] = h.astype(BF16)
        acc_ref[...] = jnp.zeros_like(acc_ref)

    h = hn_ref[...]
    g = _dot(h, wg_ref[...])
    u = _dot(h, wu_ref[...])
    a = (g * _sigmoid(g) * u).astype(BF16)
    acc_ref[...] += _dot(a, w2_ref[...])

    @pl.when(j == nj - 1)
    def _():
        o_ref[...] = x_ref[...] + 0.5 * mod_ref[0, i_mod + 2:i_mod + 3, :] * acc_ref[...]


def _ffn(x, mod, nw, w1, w2, i_mod):
    nj = D_FF_PAD // FF_TILE
    return pl.pallas_call(
        functools.partial(_ffn_kernel, i_mod=i_mod, nj=nj),
        grid=(N_TOK // FFN_TM, nj),
        in_specs=[
            pl.BlockSpec((FFN_TM, D_MODEL), lambda i, j: (i, 0)),
            pl.BlockSpec((1, N_MOD, D_MODEL), lambda i, j: (_mod_row(i, FFN_TM), 0, 0)),
            pl.BlockSpec((1, D_MODEL), lambda i, j: (0, 0)),
            pl.BlockSpec((D_MODEL, FF_TILE), lambda i, j: (0, j)),
            pl.BlockSpec((D_MODEL, FF_TILE), lambda i, j: (0, nj + j)),
            pl.BlockSpec((FF_TILE, D_MODEL), lambda i, j: (j, 0)),
        ],
        out_specs=pl.BlockSpec((FFN_TM, D_MODEL), lambda i, j: (i, 0)),
        out_shape=jax.ShapeDtypeStruct((N_TOK, D_MODEL), F32),
        scratch_shapes=[pltpu.VMEM((FFN_TM, D_MODEL), BF16), pltpu.VMEM((FFN_TM, D_MODEL), F32)],
        name="ffn",
        compiler_params=pltpu.CompilerParams(
            dimension_semantics=("parallel", "arbitrary"), vmem_limit_bytes=VMEM_LIMIT),
    )(x, mod, nw.reshape(1, D_MODEL), w1, w1, w2)


def _mixer_in_kernel(x_ref, mod_ref, nw_ref, w_ref, wt_ref, o_ref, t_ref, hn_ref):
    j = pl.program_id(1)

    @pl.when(j == 0)
    def _():
        h = _norm_modulate(x_ref[...], nw_ref[...], mod_ref[0, 4:5, :], mod_ref[0, 3:4, :]).astype(BF16)
        hn_ref[...] = h
        t_ref[...] = _dot(h, wt_ref[...])

    o_ref[...] = _dot(hn_ref[...], w_ref[...])


def _mixer_in(x, mod, nw, w_main, w_tail):
    return pl.pallas_call(
        _mixer_in_kernel,
        grid=(N_TOK // IN_TM, N_MAIN // IN_TN),
        in_specs=[
            pl.BlockSpec((IN_TM, D_MODEL), lambda i, j: (i, 0)),
            pl.BlockSpec((1, N_MOD, D_MODEL), lambda i, j: (_mod_row(i, IN_TM), 0, 0)),
            pl.BlockSpec((1, D_MODEL), lambda i, j: (0, 0)),
            pl.BlockSpec((D_MODEL, IN_TN), lambda i, j: (0, j)),
            pl.BlockSpec((D_MODEL, TAIL_W), lambda i, j: (0, 0)),
        ],
        out_specs=[
            pl.BlockSpec((IN_TM, IN_TN), lambda i, j: (i, j)),
            pl.BlockSpec((IN_TM, TAIL_W), lambda i, j: (i, 0)),
        ],
        out_shape=[jax.ShapeDtypeStruct((N_TOK, N_MAIN), F32),
                   jax.ShapeDtypeStruct((N_TOK, TAIL_W), F32)],
        scratch_shapes=[pltpu.VMEM((IN_TM, D_MODEL), BF16)],
        name="mixer_in",
        compiler_params=pltpu.CompilerParams(
            dimension_semantics=("parallel", "arbitrary"), vmem_limit_bytes=VMEM_LIMIT),
    )(x, mod, nw.reshape(1, D_MODEL), w_main, w_tail)


def _seq_pos(g, n_ctx_blk, ctx_len, lat_len, reverse):
    is_ctx = g < n_ctx_blk
    gl = jnp.maximum(g - n_ctx_blk, 0)
    seq = jnp.where(is_ctx, g // ctx_len, gl // lat_len)
    pos = jnp.where(is_ctx, g % ctx_len, gl % lat_len)
    n = jnp.where(is_ctx, ctx_len, lat_len)
    base = jnp.where(is_ctx, seq * ctx_len, n_ctx_blk + seq * lat_len)
    blk = base + ((n - 1 - pos) if reverse else pos)
    return is_ctx, seq, pos, n, blk


def _tri_mask(n, reverse):
    row = lax.broadcasted_iota(jnp.int32, (n, n), 0)
    col = lax.broadcasted_iota(jnp.int32, (n, n), 1)
    return (row <= col) if reverse else (row >= col)


def _hgrn2_kernel(q_ref, v_ref, f_ref, lb_ref, llb_ref, s0_ref, o_ref, fin_ref, st_ref, *, reverse):
    c = HG_C
    nch = SCAN_TB // c
    is_ctx, _, pos, seq_len, _ = _seq_pos(pl.program_id(0), N_PROMPT // SCAN_TB, SEQ // SCAN_TB,
                                          DEC_SEQ // SCAN_TB, reverse)

    @pl.when(pos == 0)
    def _():
        for h in range(HEADS):
            st_ref[h] = jnp.where(is_ctx, 0.0, s0_ref[0, 0, 0, h].T)

    incl = _tri_mask(c, reverse)
    tri = jnp.where(incl, 1.0, 0.0).astype(BF16)
    mid = c // 2
    hcs = [slice(h * HEAD_DIM, (h + 1) * HEAD_DIM) for h in range(HEADS)]

    def chunk(n, carry):
        nn = (nch - 1 - n) if reverse else n
        r0 = pl.multiple_of(nn * c, c)
        rows = pl.ds(r0, c)
        fr = f_ref[rows, :]
        qr = q_ref[rows, :]
        vb = v_ref[rows, :].astype(BF16)
        lb = lb_ref[...]
        llb = llb_ref[...]
        b = jnp.log1p(-lb) + jnp.minimum(fr, 0.0) - jnp.log1p(jnp.exp(-jnp.abs(fr)))
        m = jnp.maximum(llb, b)
        logf = m + jnp.log(jnp.exp(llb - m) + jnp.exp(b - m))
        k = (1.0 - lb) * _sigmoid(-fr)
        q = qr * _sigmoid(qr) * (HEAD_DIM ** -0.5)
        bc = _dot_exact_lhs(tri, logf)
        btot = jnp.sum(logf, axis=0, keepdims=True)
        r = bc[mid:mid + 1, :]
        q0 = (q * jnp.exp(bc)).astype(BF16)
        qm = (q * jnp.exp(jnp.minimum(bc - r, EXP_CLAMP))).astype(BF16)
        km = (k * jnp.exp(jnp.minimum(r - bc, EXP_CLAMP))).astype(BF16)
        kl = (k * jnp.exp(btot - bc)).astype(BF16)
        dec = jnp.exp(btot)
        att = [jnp.where(incl, _dot_nt(qm[:, hc], km[:, hc]), 0.0).astype(BF16) for hc in hcs]
        st = [st_ref[h] for h in range(HEADS)]
        intra = [_dot(att[h], vb[:, hcs[h]]) for h in range(HEADS)]
        inter = [_dot_nt(q0[:, hcs[h]], st[h].astype(BF16)) for h in range(HEADS)]
        upd = [_dot_tn(vb[:, hc], kl[:, hc]) for hc in hcs]
        for h in range(HEADS):
            o_ref[rows, hcs[h]] = intra[h] + inter[h]
            st_ref[h] = st[h] * dec[:, hcs[h]] + upd[h]
        return carry

    lax.fori_loop(0, nch, chunk, 0)

    @pl.when(pos == seq_len - 1)
    def _():
        for h in range(HEADS):
            fin_ref[0, h] = st_ref[h].T


def _hgrn2_scan(proj, lb, llb, state, *, layer, reverse):
    d = 1 if reverse else 0
    n_ctx_blk, ctx_len, lat_len = N_PROMPT // SCAN_TB, SEQ // SCAN_TB, DEC_SEQ // SCAN_TB
    sp = lambda g: _seq_pos(g, n_ctx_blk, ctx_len, lat_len, reverse)
    tok = lambda col: pl.BlockSpec((SCAN_TB, GROUP_W), lambda g: (sp(g)[4], col))
    vec = pl.BlockSpec((1, GROUP_W), lambda g: (0, 0))
    st_blk = (1, HEADS, HEAD_DIM, HEAD_DIM)
    o, fin = pl.pallas_call(
        functools.partial(_hgrn2_kernel, reverse=reverse),
        name="hgrn2_scan",
        grid=(N_TOK // SCAN_TB,),
        in_specs=[tok(0), tok(1), tok(2 + d), vec, vec,
                  pl.BlockSpec((1, 1) + st_blk, lambda g: (jnp.where(sp(g)[0], 0, sp(g)[1]), layer, d, 0, 0, 0))],
        out_specs=[tok(0),
                   pl.BlockSpec(st_blk, lambda g: (jnp.where(sp(g)[0], sp(g)[1], BATCH), 0, 0, 0))],
        out_shape=[jax.ShapeDtypeStruct((N_TOK, GROUP_W), F32),
                   jax.ShapeDtypeStruct((BATCH + 1,) + st_blk[1:], F32)],
        scratch_shapes=[pltpu.VMEM((HEADS, HEAD_DIM, HEAD_DIM), F32)],
        compiler_params=pltpu.CompilerParams(dimension_semantics=("arbitrary",), vmem_limit_bytes=VMEM_LIMIT),
    )(proj, proj, proj, lb, llb, state)
    return o, fin[:BATCH]


def _gdn_pre_kernel(q_ref, k_ref, v_ref, cwq_ref, cwk_ref, cwv_ref, qo_ref, ko_ref, vo_ref):
    i = pl.program_id(0)
    period = jnp.where(i < N_PROMPT // SCAN_TB, SCAN_TB, GRID_W)
    pos = lax.broadcasted_iota(jnp.int32, (SCAN_TB, GROUP_W), 0) & (period - 1)

    def conv_silu(x_ref, cw_ref):
        x = x_ref[...]
        acc = x * cw_ref[CONV_K // 2:CONV_K // 2 + 1, :]
        for j in range(CONV_K):
            off = j - CONV_K // 2
            if off == 0:
                continue
            xs = pltpu.roll(x, (-off) % SCAN_TB, 0)
            valid = (pos + off >= 0) & (pos + off < period)
            acc = acc + jnp.where(valid, xs, 0.0) * cw_ref[j:j + 1, :]
        return acc * _sigmoid(acc)

    def l2norm_heads(y, o_ref, scale):
        for h in range(HEADS):
            hc = slice(h * HEAD_DIM, (h + 1) * HEAD_DIM)
            yh = y[:, hc]
            o_ref[:, hc] = yh * (lax.rsqrt(jnp.sum(yh * yh, axis=-1, keepdims=True) + EPS) * scale)

    l2norm_heads(conv_silu(q_ref, cwq_ref), qo_ref, HEAD_DIM ** -0.5)
    l2norm_heads(conv_silu(k_ref, cwk_ref), ko_ref, 1.0)
    vo_ref[...] = conv_silu(v_ref, cwv_ref)


def _gdn_pre(proj, conv_w):
    cw = [conv_w[:, i * GROUP_W:(i + 1) * GROUP_W] for i in range(3)]
    tok = lambda col: pl.BlockSpec((SCAN_TB, GROUP_W), lambda i: (i, col))
    cws = pl.BlockSpec((CONV_K, GROUP_W), lambda i: (0, 0))
    out = jax.ShapeDtypeStruct((N_TOK, GROUP_W), F32)
    return pl.pallas_call(
        _gdn_pre_kernel,
        grid=(N_TOK // SCAN_TB,),
        in_specs=[tok(5), tok(6), tok(7), cws, cws, cws],
        out_specs=[tok(0), tok(0), tok(0)],
        out_shape=[out, out, out],
        name="gdn_pre",
        compiler_params=pltpu.CompilerParams(
            dimension_semantics=("parallel",), vmem_limit_bytes=VMEM_LIMIT),
    )(proj, proj, proj, *cw)


def _gdn_kernel(q_ref, k_ref, v_ref, ab_ref, abt_ref, arow_ref, drow_ref, acol_ref, dcol_ref, s0_ref,
                o_ref, fin_ref, s_ref, *, reverse, d):
    c = GD_C
    is_ctx, _, pos, seq_len, _ = _seq_pos(pl.program_id(0), N_PROMPT // c, SEQ // c, DEC_SEQ // c, reverse)

    @pl.when(pos == 0)
    def _():
        for h in range(HEADS):
            s_ref[h] = jnp.where(is_ctx, 0.0, s0_ref[0, 0, 0, h])

    incl = _tri_mask(c, reverse)
    incl_t = _tri_mask(c, not reverse)
    row = lax.broadcasted_iota(jnp.int32, (c, c), 0)
    col = lax.broadcasted_iota(jnp.int32, (c, c), 1)
    strict = incl & (row != col)
    eye = (row == col).astype(F32)
    same_blk = {lvl: (row >> lvl) == (col >> lvl) for lvl in range(3, c.bit_length())}

    ab = ab_ref[...]
    g_tok = -jnp.exp(arow_ref[...]) * _softplus(ab + drow_ref[...])
    beta_tok = _sigmoid(ab)
    gcum = _dot_exact_lhs(jnp.where(incl, 1.0, 0.0).astype(BF16), g_tok)
    g_head = -jnp.exp(acol_ref[...]) * _softplus(abt_ref[0] + dcol_ref[...])
    gcum_t = _dot_exact_rhs(g_head, jnp.where(incl_t, 1.0, 0.0).astype(BF16))
    last = 0 if reverse else c - 1

    hcs = [slice(h * HEAD_DIM, (h + 1) * HEAD_DIM) for h in range(HEADS)]
    ia = [d * HEADS + h for h in range(HEADS)]
    ib = [2 * HEADS + d * HEADS + h for h in range(HEADS)]
    bf = lambda xs: [x.astype(BF16) for x in xs]
    qn = [q_ref[:, hc] for hc in hcs]
    kn = [k_ref[:, hc] for hc in hcs]
    vv = [v_ref[:, hc] for hc in hcs]
    gcol = [gcum[:, i:i + 1] for i in ia]
    grow = [gcum_t[i:i + 1, :] for i in ia]
    bcol = [beta_tok[:, i:i + 1] for i in ib]
    glast = [gcum[last:last + 1, i:i + 1] for i in ia]
    decay = _each(lambda gc, gr: jnp.exp(jnp.where(incl, gc - gr, -1e30)), gcol, grow)
    knb = bf(kn)
    kb = _each(jnp.multiply, kn, bcol)
    kk = _each(_dot_nt, bf(kb), knb)
    a = _each(lambda x, dc: jnp.where(strict, x * dc, 0.0), kk, decay)
    bdot = lambda xs, ys: _each(_dot, bf(xs), bf(ys))
    p = [jnp.where(same_blk[3], -x, 0.0) for x in a]
    t = [eye + x for x in p]
    p = bdot(p, p)
    t = _each(jnp.add, t, bdot(t, p))
    p = bdot(p, p)
    t = _each(jnp.add, t, bdot(t, p))
    for lvl in range(3, GD_C.bit_length() - 1):
        couple = same_blk[lvl + 1] & jnp.logical_not(same_blk[lvl])
        n = [jnp.where(couple, x, 0.0) for x in a]
        t = _each(jnp.subtract, t, bdot(bdot(t, n), t))
    rhs = _each(lambda v_, b_, kb_, gc: jnp.concatenate([v_ * b_, kb_ * jnp.exp(gc)], axis=-1), vv, bcol, kb, gcol)
    sol = bdot(t, rhs)
    qk = _each(_dot_nt, bf(qn), knb)
    aqk = bf(_each(jnp.multiply, qk, decay))
    s = [s_ref[h] for h in range(HEADS)]
    sb = bf(s)
    ws = _each(_dot, bf([x[:, HEAD_DIM:] for x in sol]), sb)
    vnb = bf(_each(lambda x, y: x[:, :HEAD_DIM] - y, sol, ws))
    qs = _each(_dot, bf(_each(lambda q_, gc: q_ * jnp.exp(gc), qn, gcol)), sb)
    av = _each(_dot, aqk, vnb)
    for h in range(HEADS):
        o_ref[:, hcs[h]] = qs[h] + av[h]
    kdec = bf(_each(lambda k_, gl, gc: k_ * jnp.exp(gl - gc), kn, glast, gcol))
    kv = _each(_dot_tn, kdec, vnb)
    for h in range(HEADS):
        s_ref[h] = s[h] * jnp.exp(glast[h]) + kv[h]

    @pl.when(pos == seq_len - 1)
    def _():
        for h in range(HEADS):
            fin_ref[0, h] = s_ref[h]


def _gdn_scan(qn, kn, vc, tail, tail_t, arow, drow, acol, dcol, state, *, layer, reverse):
    d = 1 if reverse else 0
    c = GD_C
    sp = lambda g: _seq_pos(g, N_PROMPT // c, SEQ // c, DEC_SEQ // c, reverse)
    tok = pl.BlockSpec((c, GROUP_W), lambda g: (sp(g)[4], 0))
    vec_row = pl.BlockSpec((1, TAIL_W), lambda g: (0, 0))
    vec_col = pl.BlockSpec((TAIL_W, 1), lambda g: (0, 0))
    st_blk = (1, HEADS, HEAD_DIM, HEAD_DIM)
    o, fin = pl.pallas_call(
        functools.partial(_gdn_kernel, reverse=reverse, d=d),
        name="gdn_scan",
        grid=(N_TOK // c,),
        in_specs=[tok, tok, tok,
                  pl.BlockSpec((c, TAIL_W), lambda g: (sp(g)[4], 0)),
                  pl.BlockSpec((1, TAIL_W, c), lambda g: (sp(g)[4], 0, 0)),
                  vec_row, vec_row, vec_col, vec_col,
                  pl.BlockSpec((1, 1) + st_blk, lambda g: (jnp.where(sp(g)[0], 0, sp(g)[1]), layer, d, 0, 0, 0))],
        out_specs=[tok,
                   pl.BlockSpec(st_blk, lambda g: (jnp.where(sp(g)[0], sp(g)[1], BATCH), 0, 0, 0))],
        out_shape=[jax.ShapeDtypeStruct((N_TOK, GROUP_W), F32),
                   jax.ShapeDtypeStruct((BATCH + 1,) + st_blk[1:], F32)],
        scratch_shapes=[pltpu.VMEM((HEADS, HEAD_DIM, HEAD_DIM), F32)],
        compiler_params=pltpu.CompilerParams(dimension_semantics=("arbitrary",), vmem_limit_bytes=VMEM_LIMIT),
    )(qn, kn, vc, tail, tail_t, arow, drow, acol, dcol, state)
    return o, fin[:BATCH]


def _mixer_out_kernel(x_ref, mod_ref, ohf_ref, ohb_ref, ogf_ref, ogb_ref, hgate_ref, ggate_ref,
                      hw_ref, gw_ref, wo_ref, o_ref, cat_ref):
    def norm_gate(of_ref, ob_ref, gate_ref, w_ref, base):
        for h in range(HEADS):
            hc = slice(h * HEAD_DIM, (h + 1) * HEAD_DIM)
            o = of_ref[:, hc] + ob_ref[:, hc]
            on = o * lax.rsqrt(jnp.mean(o * o, axis=-1, keepdims=True) + EPS) * w_ref[...]
            gate = gate_ref[:, hc]
            cat_ref[:, base + h * HEAD_DIM:base + (h + 1) * HEAD_DIM] = (on * (gate * _sigmoid(gate))).astype(BF16)

    norm_gate(ohf_ref, ohb_ref, hgate_ref, hw_ref, 0)
    norm_gate(ogf_ref, ogb_ref, ggate_ref, gw_ref, GROUP_W)
    o_ref[...] = x_ref[...] + mod_ref[0, 5:6, :] * _dot(cat_ref[...], wo_ref[...])


def _mixer_out(x, mod, o_hg_f, o_hg_b, o_gd_f, o_gd_b, proj, hg_norm_w, gd_norm_w, w_out):
    tok = lambda col: pl.BlockSpec((OUT_TM, GROUP_W), lambda i: (i, col))
    vec = pl.BlockSpec((1, HEAD_DIM), lambda i: (0, 0))
    return pl.pallas_call(
        _mixer_out_kernel,
        grid=(N_TOK // OUT_TM,),
        in_specs=[
            pl.BlockSpec((OUT_TM, D_MODEL), lambda i: (i, 0)),
            pl.BlockSpec((1, N_MOD, D_MODEL), lambda i: (_mod_row(i, OUT_TM), 0, 0)),
            tok(0), tok(0), tok(0), tok(0), tok(4), tok(8),
            vec, vec,
            pl.BlockSpec((D_MODEL, D_MODEL), lambda i: (0, 0)),
        ],
        out_specs=pl.BlockSpec((OUT_TM, D_MODEL), lambda i: (i, 0)),
        out_shape=jax.ShapeDtypeStruct((N_TOK, D_MODEL), F32),
        scratch_shapes=[pltpu.VMEM((OUT_TM, D_MODEL), BF16)],
        name="mixer_out",
        compiler_params=pltpu.CompilerParams(
            dimension_semantics=("parallel",), vmem_limit_bytes=VMEM_LIMIT),
    )(x, mod, o_hg_f, o_hg_b, o_gd_f, o_gd_b, proj, proj,
      hg_norm_w.reshape(1, HEAD_DIM), gd_norm_w.reshape(1, HEAD_DIM), w_out)


def _final_norm_kernel(x_ref, w_ref, o_ref):
    x = x_ref[...]
    o_ref[...] = x * lax.rsqrt(jnp.mean(x * x, axis=-1, keepdims=True) + EPS) * w_ref[...]


def _final_norm(x, w, row0, n_rows):
    blk0 = row0 // NORM_TM
    return pl.pallas_call(
        _final_norm_kernel,
        grid=(n_rows // NORM_TM,),
        in_specs=[pl.BlockSpec((NORM_TM, D_MODEL), lambda i: (blk0 + i, 0)),
                  pl.BlockSpec((1, D_MODEL), lambda i: (0, 0))],
        out_specs=pl.BlockSpec((NORM_TM, D_MODEL), lambda i: (i, 0)),
        out_shape=jax.ShapeDtypeStruct((n_rows, D_MODEL), F32),
        name="final_norm",
        compiler_params=pltpu.CompilerParams(
            dimension_semantics=("parallel",), vmem_limit_bytes=VMEM_LIMIT),
    )(x, w.reshape(1, D_MODEL))


def _pad_cols(w, n):
    return jnp.pad(w, ((0, 0), (0, n - w.shape[1])))


def _lane_vec(v):
    return jnp.pad(v.reshape(-1).astype(F32), (0, TAIL_W - 2 * HEADS))


def kernel(x_prompt, x_sample, c, state_hgrn2, state_gdn, c_ctx, norm_w, w_mod, b_mod, ffn_w_in, ffn_w_out,
           w_in, hg_lower_bounds, hg_norm_w, gd_conv_w, gd_A_log, gd_dt_bias, gd_norm_w, w_out, final_norm_w):
    x = jnp.concatenate([x_prompt.reshape(N_PROMPT, D_MODEL), x_sample.reshape(N_SAMPLE, D_MODEL)], axis=0)
    cond = jnp.concatenate([c_ctx[None, :], c, jnp.zeros((N_COND - 1 - DEC_BATCH, D_MODEL), F32)], axis=0)
    mods = _modulation(cond, w_mod, b_mod)

    cs = jnp.cumsum(jax.nn.softmax(hg_lower_bounds.astype(F32), axis=1), axis=1)
    lbs = cs - cs[:, :1]

    def ffn_half(x, l, k):
        w1 = jnp.concatenate([_pad_cols(ffn_w_in[l, k][:, :D_FF], D_FF_PAD),
                              _pad_cols(ffn_w_in[l, k][:, D_FF:], D_FF_PAD)], axis=1).astype(BF16)
        w2 = jnp.pad(ffn_w_out[l, k], ((0, D_FF_PAD - D_FF), (0, 0))).astype(BF16)
        return _ffn(x, mods[l], norm_w[l, 2 * k], w1, w2, 6 * k)

    def mixer(x, l):
        w_main = w_in[l][:, :N_MAIN].astype(BF16)
        w_tail = _pad_cols(w_in[l][:, N_MAIN:], TAIL_W).astype(BF16)
        proj, tail = _mixer_in(x, mods[l], norm_w[l, 1], w_main, w_tail)
        tail_t = tail.reshape(N_TOK // GD_C, GD_C, TAIL_W).transpose(0, 2, 1)
        qn, kn, vc = _gdn_pre(proj, gd_conv_w[l])
        arow = _lane_vec(gd_A_log[l])
        drow = _lane_vec(gd_dt_bias[l])
        gd_args = (qn, kn, vc, tail, tail_t, arow.reshape(1, TAIL_W), drow.reshape(1, TAIL_W),
                   arow.reshape(TAIL_W, 1), drow.reshape(TAIL_W, 1))
        o_hg, o_gd, fin_hg, fin_gd = [], [], [], []
        for d in range(2):
            lb = lbs[d, l].reshape(1, GROUP_W)
            o, fin = _hgrn2_scan(proj, lb, jnp.log(lb), state_hgrn2, layer=l, reverse=d == 1)
            o_hg.append(o)
            fin_hg.append(fin)
            o, fin = _gdn_scan(*gd_args, state_gdn, layer=l, reverse=d == 1)
            o_gd.append(o)
            fin_gd.append(fin)
        x = _mixer_out(x, mods[l], o_hg[0], o_hg[1], o_gd[0], o_gd[1], proj,
                       hg_norm_w[l], gd_norm_w[l], w_out[l].astype(BF16))
        return x, jnp.stack(fin_hg, axis=1), jnp.stack(fin_gd, axis=1)

    new_hg, new_gd = [], []
    for l in range(DEPTH):
        x = ffn_half(x, l, 0)
        x, s_hg, s_gd = mixer(x, l)
        new_hg.append(s_hg)
        new_gd.append(s_gd)
        x = ffn_half(x, l, 1)

    y_prompt = _final_norm(x, final_norm_w, 0, N_PROMPT).reshape(BATCH, SEQ, D_MODEL)
    y_sample = _final_norm(x, final_norm_w, N_PROMPT, N_SAMPLE).reshape(DEC_BATCH, DEC_SEQ, D_MODEL)
    return (y_prompt, y_sample, jnp.stack(new_hg, axis=1), jnp.stack(new_gd, axis=1))
```

```python
import functools

import jax
import jax.numpy as jnp
from jax import lax
from jax.experimental import pallas as pl
from jax.experimental.pallas import tpu as pltpu

F32 = jnp.float32
BF16 = jnp.bfloat16

D_MODEL = 2048
BATCH = 32
SEQ = 256
DEPTH = 2
DEC_BATCH = 8
DEC_SEQ = 2048
GRID_W = 64
HEADS = 8
HEAD_DIM = 128
GROUP_W = HEADS * HEAD_DIM
CONV_K = 5
D_FF = 5504
N_MOD = 9
EPS = 1e-6

N_PROMPT = BATCH * SEQ
N_SAMPLE = DEC_BATCH * DEC_SEQ
N_TOK = N_PROMPT + N_SAMPLE
N_COND = 16
N_MAIN = 9 * GROUP_W
TAIL_W = 128

FF_TILE = 512
D_FF_PAD = 5632
FFN_TM = 512
IN_TM = 1024
IN_TN = 1024
OUT_TM = 256
NORM_TM = 1024
MOD_TN = 1024
SCAN_TB = 256
HG_C = 32
HG_GROUP = 4
GD_C = 64
GD_TB = 128
EXP_CLAMP = 80.0
VMEM_LIMIT = 52 * 1024 * 1024


def _mod_row(i, tm):
    n_ctx = N_PROMPT // tm
    return jnp.where(i < n_ctx, 0, 1 + (i - n_ctx) // (DEC_SEQ // tm))


def _sigmoid(x):
    return jax.nn.sigmoid(x)


def _softplus(x):
    return jnp.maximum(x, 0.0) + jnp.log1p(jnp.exp(-jnp.abs(x)))


def _dot(a, b):
    return jnp.dot(a, b, preferred_element_type=F32)


def _dot_nt(a, b):
    return lax.dot_general(a, b, (((1,), (1,)), ((), ())), preferred_element_type=F32)


def _dot_tn(a, b):
    return lax.dot_general(a, b, (((0,), (0,)), ((), ())), preferred_element_type=F32)


def _split3_bf16(x):
    hi = x.astype(BF16)
    r1 = x - hi.astype(F32)
    mid = r1.astype(BF16)
    return hi, mid, (r1 - mid.astype(F32)).astype(BF16)


def _dot_exact_lhs(a, x):
    hi, mid, lo = _split3_bf16(x)
    return _dot(a, hi) + (_dot(a, mid) + _dot(a, lo))


def _dot_exact_rhs(x, a):
    hi, mid, lo = _split3_bf16(x)
    return _dot(hi, a) + (_dot(mid, a) + _dot(lo, a))


def _each(f, *lists):
    return [f(*args) for args in zip(*lists)]


def _norm_modulate(x, nw, scale, shift):
    ms = jnp.mean(x * x, axis=-1, keepdims=True)
    return x * lax.rsqrt(ms + EPS) * nw * (1.0 + scale) + shift


def _mod_kernel(c_ref, w_ref, b_ref, o_ref):
    c = c_ref[...]
    s = (c * _sigmoid(c)).astype(BF16)
    o_ref[0] = _dot(s, w_ref[0].astype(BF16)) + b_ref[0]


def _modulation(cond, w_mod, b_mod):
    n_out = N_MOD * D_MODEL
    out = pl.pallas_call(
        _mod_kernel,
        grid=(DEPTH, n_out // MOD_TN),
        in_specs=[
            pl.BlockSpec((N_COND, D_MODEL), lambda l, j: (0, 0)),
            pl.BlockSpec((1, D_MODEL, MOD_TN), lambda l, j: (l, 0, j)),
            pl.BlockSpec((1, 1, MOD_TN), lambda l, j: (l, 0, j)),
        ],
        out_specs=pl.BlockSpec((1, N_COND, MOD_TN), lambda l, j: (l, 0, j)),
        out_shape=jax.ShapeDtypeStruct((DEPTH, N_COND, n_out), F32),
        name="modulation",
        compiler_params=pltpu.CompilerParams(
            dimension_semantics=("arbitrary", "arbitrary"), vmem_limit_bytes=VMEM_LIMIT),
    )(cond, w_mod, b_mod.reshape(DEPTH, 1, n_out))
    return out.reshape(DEPTH, N_COND, N_MOD, D_MODEL)


def _ffn_kernel(x_ref, mod_ref, nw_ref, wg_ref, wu_ref, w2_ref, o_ref, hn_ref, acc_ref, *, i_mod, nj):
    j = pl.program_id(1)

    @pl.when(j == 0)
    def _():
        h = _norm_modulate(x_ref[...], nw_ref[...],
                           mod_ref[0, i_mod + 1:i_mod + 2, :], mod_ref[0, i_mod:i_mod + 1, :])
        hn_ref[...] = h.astype(BF16)
        acc_ref[...] = jnp.zeros_like(acc_ref)

    h = hn_ref[...]
    g = _dot(h, wg_ref[...])
    u = _dot(h, wu_ref[...])
    a = (g * _sigmoid(g) * u).astype(BF16)
    acc_ref[...] += _dot(a, w2_ref[...])

    @pl.when(j == nj - 1)
    def _():
        o_ref[...] = x_ref[...] + 0.5 * mod_ref[0, i_mod + 2:i_mod + 3, :] * acc_ref[...]


def _ffn(x, mod, nw, wg, wu, w2, i_mod):
    nj = D_FF_PAD // FF_TILE
    return pl.pallas_call(
        functools.partial(_ffn_kernel, i_mod=i_mod, nj=nj),
        grid=(N_TOK // FFN_TM, nj),
        in_specs=[
            pl.BlockSpec((FFN_TM, D_MODEL), lambda i, j: (i, 0)),
            pl.BlockSpec((1, N_MOD, D_MODEL), lambda i, j: (_mod_row(i, FFN_TM), 0, 0)),
            pl.BlockSpec((1, D_MODEL), lambda i, j: (0, 0)),
            pl.BlockSpec((D_MODEL, FF_TILE), lambda i, j: (0, j)),
            pl.BlockSpec((D_MODEL, FF_TILE), lambda i, j: (0, j)),
            pl.BlockSpec((FF_TILE, D_MODEL), lambda i, j: (j, 0)),
        ],
        out_specs=pl.BlockSpec((FFN_TM, D_MODEL), lambda i, j: (i, 0)),
        out_shape=jax.ShapeDtypeStruct((N_TOK, D_MODEL), F32),
        scratch_shapes=[pltpu.VMEM((FFN_TM, D_MODEL), BF16), pltpu.VMEM((FFN_TM, D_MODEL), F32)],
        name="ffn",
        compiler_params=pltpu.CompilerParams(
            dimension_semantics=("parallel", "arbitrary"), vmem_limit_bytes=VMEM_LIMIT),
    )(x, mod, nw.reshape(1, D_MODEL), wg, wu, w2)


def _mixer_in_kernel(x_ref, mod_ref, nw_ref, w_ref, wt_ref, o_ref, t_ref, hn_ref):
    j = pl.program_id(1)

    @pl.when(j == 0)
    def _():
        h = _norm_modulate(x_ref[...], nw_ref[...], mod_ref[0, 4:5, :], mod_ref[0, 3:4, :]).astype(BF16)
        hn_ref[...] = h
        t_ref[...] = _dot(h, wt_ref[...])

    o_ref[...] = _dot(hn_ref[...], w_ref[...])


def _mixer_in(x, mod, nw, w_main, w_tail):
    return pl.pallas_call(
        _mixer_in_kernel,
        grid=(N_TOK // IN_TM, N_MAIN // IN_TN),
        in_specs=[
            pl.BlockSpec((IN_TM, D_MODEL), lambda i, j: (i, 0)),
            pl.BlockSpec((1, N_MOD, D_MODEL), lambda i, j: (_mod_row(i, IN_TM), 0, 0)),
            pl.BlockSpec((1, D_MODEL), lambda i, j: (0, 0)),
            pl.BlockSpec((D_MODEL, IN_TN), lambda i, j: (0, j)),
            pl.BlockSpec((D_MODEL, TAIL_W), lambda i, j: (0, 0)),
        ],
        out_specs=[
            pl.BlockSpec((IN_TM, IN_TN), lambda i, j: (i, j)),
            pl.BlockSpec((IN_TM, TAIL_W), lambda i, j: (i, 0)),
        ],
        out_shape=[jax.ShapeDtypeStruct((N_TOK, N_MAIN), F32),
                   jax.ShapeDtypeStruct((N_TOK, TAIL_W), F32)],
        scratch_shapes=[pltpu.VMEM((IN_TM, D_MODEL), BF16)],
        name="mixer_in",
        compiler_params=pltpu.CompilerParams(
            dimension_semantics=("parallel", "arbitrary"), vmem_limit_bytes=VMEM_LIMIT),
    )(x, mod, nw.reshape(1, D_MODEL), w_main, w_tail)


def _seq_pos(g, n_ctx_blk, ctx_len, lat_len, reverse):
    is_ctx = g < n_ctx_blk
    gl = jnp.maximum(g - n_ctx_blk, 0)
    seq = jnp.where(is_ctx, g // ctx_len, gl // lat_len)
    pos = jnp.where(is_ctx, g % ctx_len, gl % lat_len)
    n = jnp.where(is_ctx, ctx_len, lat_len)
    base = jnp.where(is_ctx, seq * ctx_len, n_ctx_blk + seq * lat_len)
    blk = base + ((n - 1 - pos) if reverse else pos)
    return is_ctx, seq, pos, n, blk


def _tri_mask(n, reverse):
    row = lax.broadcasted_iota(jnp.int32, (n, n), 0)
    col = lax.broadcasted_iota(jnp.int32, (n, n), 1)
    return (row <= col) if reverse else (row >= col)


def _hgrn2_kernel(q_ref, v_ref, f_ref, lb_ref, llb_ref, s0_ref, o_ref, fin_ref, st_ref, *, reverse):
    c = HG_C
    nch = SCAN_TB // c
    is_ctx, _, pos, seq_len, _ = _seq_pos(pl.program_id(0), N_PROMPT // SCAN_TB, SEQ // SCAN_TB,
                                          DEC_SEQ // SCAN_TB, reverse)

    @pl.when(pos == 0)
    def _():
        for h in range(HEADS):
            st_ref[h] = jnp.where(is_ctx, 0.0, s0_ref[0, 0, 0, h].T)

    incl = _tri_mask(c, reverse)
    tri = jnp.where(incl, 1.0, 0.0).astype(BF16)
    mid = c // 2
    hcs = [slice(h * HEAD_DIM, (h + 1) * HEAD_DIM) for h in range(HEADS)]

    lb = lb_ref[...]
    llb = llb_ref[...]

    def gates(rows):
        fr = f_ref[rows, :]
        qr = q_ref[rows, :]
        b = jnp.log1p(-lb) + jnp.minimum(fr, 0.0) - jnp.log1p(jnp.exp(-jnp.abs(fr)))
        m = jnp.maximum(llb, b)
        logf = m + jnp.log(jnp.exp(llb - m) + jnp.exp(b - m))
        k = (1.0 - lb) * _sigmoid(-fr)
        q = qr * _sigmoid(qr) * (HEAD_DIM ** -0.5)
        return logf, k, q

    def scaled(logf, k, q, bc):
        btot = jnp.sum(logf, axis=0, keepdims=True)
        r = bc[mid:mid + 1, :]
        q0 = (q * jnp.exp(bc)).astype(BF16)
        qm = (q * jnp.exp(jnp.minimum(bc - r, EXP_CLAMP))).astype(BF16)
        km = (k * jnp.exp(jnp.minimum(r - bc, EXP_CLAMP))).astype(BF16)
        kl = (k * jnp.exp(btot - bc)).astype(BF16)
        return q0, qm, km, kl, jnp.exp(btot)

    order = list(reversed(range(nch))) if reverse else list(range(nch))
    st = [st_ref[h] for h in range(HEADS)]
    for g0 in range(0, nch, HG_GROUP):
        rows = [slice(nn * c, (nn + 1) * c) for nn in order[g0:g0 + HG_GROUP]]
        gate = [gates(r) for r in rows]
        bc = [_dot_exact_lhs(tri, gt[0]) for gt in gate]
        sc = [scaled(*gt, b_) for gt, b_ in zip(gate, bc)]
        vb = [v_ref[r, :].astype(BF16) for r in rows]
        att = [[jnp.where(incl, _dot_nt(s_[1][:, hc], s_[2][:, hc]), 0.0).astype(BF16) for hc in hcs] for s_ in sc]
        intra = [[_dot(a_[h], v_[:, hcs[h]]) for h in range(HEADS)] for a_, v_ in zip(att, vb)]
        upd = [[_dot_tn(v_[:, hc], s_[3][:, hc]) for hc in hcs] for s_, v_ in zip(sc, vb)]
        for i, r in enumerate(rows):
            q0, dec = sc[i][0], sc[i][4]
            inter = [_dot_nt(q0[:, hcs[h]], st[h].astype(BF16)) for h in range(HEADS)]
            for h in range(HEADS):
                o_ref[r, hcs[h]] = intra[i][h] + inter[h]
            st = [st[h] * dec[:, hcs[h]] + upd[i][h] for h in range(HEADS)]
    for h in range(HEADS):
        st_ref[h] = st[h]

    @pl.when(pos == seq_len - 1)
    def _():
        for h in range(HEADS):
            fin_ref[0, h] = st_ref[h].T


def _hgrn2_scan(proj, lb, llb, state, *, layer, reverse):
    d = 1 if reverse else 0
    n_ctx_blk, ctx_len, lat_len = N_PROMPT // SCAN_TB, SEQ // SCAN_TB, DEC_SEQ // SCAN_TB
    sp = lambda g: _seq_pos(g, n_ctx_blk, ctx_len, lat_len, reverse)
    tok = lambda col: pl.BlockSpec((SCAN_TB, GROUP_W), lambda g: (sp(g)[4], col))
    vec = pl.BlockSpec((1, GROUP_W), lambda g: (0, 0))
    st_blk = (1, HEADS, HEAD_DIM, HEAD_DIM)
    o, fin = pl.pallas_call(
        functools.partial(_hgrn2_kernel, reverse=reverse),
        name="hgrn2_scan",
        grid=(N_TOK // SCAN_TB,),
        in_specs=[tok(0), tok(1), tok(2 + d), vec, vec,
                  pl.BlockSpec((1, 1) + st_blk, lambda g: (jnp.where(sp(g)[0], 0, sp(g)[1]), layer, d, 0, 0, 0))],
        out_specs=[tok(0),
                   pl.BlockSpec(st_blk, lambda g: (jnp.where(sp(g)[0], sp(g)[1], BATCH), 0, 0, 0))],
        out_shape=[jax.ShapeDtypeStruct((N_TOK, GROUP_W), F32),
                   jax.ShapeDtypeStruct((BATCH + 1,) + st_blk[1:], F32)],
        scratch_shapes=[pltpu.VMEM((HEADS, HEAD_DIM, HEAD_DIM), F32)],
        compiler_params=pltpu.CompilerParams(dimension_semantics=("arbitrary",), vmem_limit_bytes=VMEM_LIMIT),
    )(proj, proj, proj, lb, llb, state)
    return o, fin[:BATCH]


def _gdn_pre_kernel(q_ref, k_ref, v_ref, cwq_ref, cwk_ref, cwv_ref, qo_ref, ko_ref, vo_ref):
    i = pl.program_id(0)
    period = jnp.where(i < N_PROMPT // SCAN_TB, SCAN_TB, GRID_W)
    pos = lax.broadcasted_iota(jnp.int32, (SCAN_TB, GROUP_W), 0) & (period - 1)

    def conv_silu(x_ref, cw_ref):
        x = x_ref[...]
        acc = x * cw_ref[CONV_K // 2:CONV_K // 2 + 1, :]
        for j in range(CONV_K):
            off = j - CONV_K // 2
            if off == 0:
                continue
            xs = pltpu.roll(x, (-off) % SCAN_TB, 0)
            valid = (pos + off >= 0) & (pos + off < period)
            acc = acc + jnp.where(valid, xs, 0.0) * cw_ref[j:j + 1, :]
        return acc * _sigmoid(acc)

    def l2norm_heads(y, o_ref, scale):
        for h in range(HEADS):
            hc = slice(h * HEAD_DIM, (h + 1) * HEAD_DIM)
            yh = y[:, hc]
            o_ref[:, hc] = yh * (lax.rsqrt(jnp.sum(yh * yh, axis=-1, keepdims=True) + EPS) * scale)

    l2norm_heads(conv_silu(q_ref, cwq_ref), qo_ref, HEAD_DIM ** -0.5)
    l2norm_heads(conv_silu(k_ref, cwk_ref), ko_ref, 1.0)
    vo_ref[...] = conv_silu(v_ref, cwv_ref)


def _gdn_pre(proj, conv_w):
    cw = [conv_w[:, i * GROUP_W:(i + 1) * GROUP_W] for i in range(3)]
    tok = lambda col: pl.BlockSpec((SCAN_TB, GROUP_W), lambda i: (i, col))
    cws = pl.BlockSpec((CONV_K, GROUP_W), lambda i: (0, 0))
    out = jax.ShapeDtypeStruct((N_TOK, GROUP_W), F32)
    return pl.pallas_call(
        _gdn_pre_kernel,
        grid=(N_TOK // SCAN_TB,),
        in_specs=[tok(5), tok(6), tok(7), cws, cws, cws],
        out_specs=[tok(0), tok(0), tok(0)],
        out_shape=[out, out, out],
        name="gdn_pre",
        compiler_params=pltpu.CompilerParams(
            dimension_semantics=("parallel",), vmem_limit_bytes=VMEM_LIMIT),
    )(proj, proj, proj, *cw)


def _gdn_kernel(q_ref, k_ref, v_ref, ab_ref, abt_ref, arow_ref, drow_ref, acol_ref, dcol_ref, s0_ref,
                o_ref, fin_ref, s_ref, *, reverse, d):
    c = GD_C
    ncs = GD_TB // c
    is_ctx, _, pos, seq_len, _ = _seq_pos(pl.program_id(0), N_PROMPT // GD_TB, SEQ // GD_TB,
                                          DEC_SEQ // GD_TB, reverse)

    @pl.when(pos == 0)
    def _():
        for h in range(HEADS):
            s_ref[h] = jnp.where(is_ctx, 0.0, s0_ref[0, 0, 0, h])

    incl = _tri_mask(c, reverse)
    incl_t = _tri_mask(c, not reverse)
    row = lax.broadcasted_iota(jnp.int32, (c, c), 0)
    col = lax.broadcasted_iota(jnp.int32, (c, c), 1)
    strict = incl & (row != col)
    eye = (row == col).astype(F32)
    same_blk = {lvl: (row >> lvl) == (col >> lvl) for lvl in range(3, c.bit_length())}
    tri = jnp.where(incl, 1.0, 0.0).astype(BF16)
    tri_t = jnp.where(incl_t, 1.0, 0.0).astype(BF16)
    last = 0 if reverse else c - 1
    rows = [slice(ci * c, (ci + 1) * c) for ci in range(ncs)]

    gcum, gcum_t, beta_tok = [], [], []
    for ci in range(ncs):
        ab = ab_ref[rows[ci], :]
        gcum.append(_dot_exact_lhs(tri, -jnp.exp(arow_ref[...]) * _softplus(ab + drow_ref[...])))
        g_head = -jnp.exp(acol_ref[...]) * _softplus(abt_ref[ci] + dcol_ref[...])
        gcum_t.append(_dot_exact_rhs(g_head, tri_t))
        beta_tok.append(_sigmoid(ab))

    hcs = [slice(h * HEAD_DIM, (h + 1) * HEAD_DIM) for h in range(HEADS)]
    prob = [(ci, h) for ci in range(ncs) for h in range(HEADS)]
    ia = [d * HEADS + h for h in range(HEADS)]
    ib = [2 * HEADS + d * HEADS + h for h in range(HEADS)]
    bf = lambda xs: [x.astype(BF16) for x in xs]
    qn = [q_ref[rows[ci], hcs[h]] for ci, h in prob]
    kn = [k_ref[rows[ci], hcs[h]] for ci, h in prob]
    vv = [v_ref[rows[ci], hcs[h]] for ci, h in prob]
    gcol = [gcum[ci][:, ia[h]:ia[h] + 1] for ci, h in prob]
    grow = [gcum_t[ci][ia[h]:ia[h] + 1, :] for ci, h in prob]
    bcol = [beta_tok[ci][:, ib[h]:ib[h] + 1] for ci, h in prob]
    glast = [gcum[ci][last:last + 1, ia[h]:ia[h] + 1] for ci, h in prob]
    decay = _each(lambda gc, gr: jnp.exp(jnp.where(incl, gc - gr, -1e30)), gcol, grow)
    knb = bf(kn)
    kb = _each(jnp.multiply, kn, bcol)
    kk = _each(_dot_nt, bf(kb), knb)
    a = _each(lambda x, dc: jnp.where(strict, x * dc, 0.0), kk, decay)
    bdot = lambda xs, ys: _each(_dot, bf(xs), bf(ys))
    p = [jnp.where(same_blk[3], -x, 0.0) for x in a]
    t = [eye + x for x in p]
    p = bdot(p, p)
    t = _each(jnp.add, t, bdot(t, p))
    p = bdot(p, p)
    t = _each(jnp.add, t, bdot(t, p))
    for lvl in range(3, GD_C.bit_length() - 1):
        couple = same_blk[lvl + 1] & jnp.logical_not(same_blk[lvl])
        n = [jnp.where(couple, x, 0.0) for x in a]
        t = _each(jnp.subtract, t, bdot(bdot(t, n), t))
    rhs = _each(lambda v_, b_, kb_, gc: jnp.concatenate([v_ * b_, kb_ * jnp.exp(gc)], axis=-1), vv, bcol, kb, gcol)
    sol = bdot(t, rhs)
    qk = _each(_dot_nt, bf(qn), knb)
    aqk = bf(_each(jnp.multiply, qk, decay))
    u = [x[:, :HEAD_DIM] for x in sol]
    wb = bf([x[:, HEAD_DIM:] for x in sol])
    qe = bf(_each(lambda q_, gc: q_ * jnp.exp(gc), qn, gcol))
    kdec = bf(_each(lambda k_, gl, gc: k_ * jnp.exp(gl - gc), kn, glast, gcol))
    sdec = [jnp.exp(x) for x in glast]

    s = [s_ref[h] for h in range(HEADS)]
    for ci in (reversed(range(ncs)) if reverse else range(ncs)):
        idx = [ci * HEADS + h for h in range(HEADS)]
        sb = bf(s)
        ws = [_dot(wb[i], sb[h]) for h, i in enumerate(idx)]
        vnb = bf([u[i] - ws[h] for h, i in enumerate(idx)])
        qs = [_dot(qe[i], sb[h]) for h, i in enumerate(idx)]
        av = [_dot(aqk[i], vnb[h]) for h, i in enumerate(idx)]
        kv = [_dot_tn(kdec[i], vnb[h]) for h, i in enumerate(idx)]
        for h, i in enumerate(idx):
            o_ref[rows[ci], hcs[h]] = qs[h] + av[h]
        s = [s[h] * sdec[i] + kv[h] for h, i in enumerate(idx)]
    for h in range(HEADS):
        s_ref[h] = s[h]

    @pl.when(pos == seq_len - 1)
    def _():
        for h in range(HEADS):
            fin_ref[0, h] = s_ref[h]


def _gdn_scan(qn, kn, vc, tail, tail_t, arow, drow, acol, dcol, state, *, layer, reverse):
    d = 1 if reverse else 0
    sp = lambda g: _seq_pos(g, N_PROMPT // GD_TB, SEQ // GD_TB, DEC_SEQ // GD_TB, reverse)
    tok = pl.BlockSpec((GD_TB, GROUP_W), lambda g: (sp(g)[4], 0))
    vec_row = pl.BlockSpec((1, TAIL_W), lambda g: (0, 0))
    vec_col = pl.BlockSpec((TAIL_W, 1), lambda g: (0, 0))
    st_blk = (1, HEADS, HEAD_DIM, HEAD_DIM)
    o, fin = pl.pallas_call(
        functools.partial(_gdn_kernel, reverse=reverse, d=d),
        name="gdn_scan",
        grid=(N_TOK // GD_TB,),
        in_specs=[tok, tok, tok,
                  pl.BlockSpec((GD_TB, TAIL_W), lambda g: (sp(g)[4], 0)),
                  pl.BlockSpec((GD_TB // GD_C, TAIL_W, GD_C), lambda g: (sp(g)[4], 0, 0)),
                  vec_row, vec_row, vec_col, vec_col,
                  pl.BlockSpec((1, 1) + st_blk, lambda g: (jnp.where(sp(g)[0], 0, sp(g)[1]), layer, d, 0, 0, 0))],
        out_specs=[tok,
                   pl.BlockSpec(st_blk, lambda g: (jnp.where(sp(g)[0], sp(g)[1], BATCH), 0, 0, 0))],
        out_shape=[jax.ShapeDtypeStruct((N_TOK, GROUP_W), F32),
                   jax.ShapeDtypeStruct((BATCH + 1,) + st_blk[1:], F32)],
        scratch_shapes=[pltpu.VMEM((HEADS, HEAD_DIM, HEAD_DIM), F32)],
        compiler_params=pltpu.CompilerParams(dimension_semantics=("arbitrary",), vmem_limit_bytes=VMEM_LIMIT),
    )(qn, kn, vc, tail, tail_t, arow, drow, acol, dcol, state)
    return o, fin[:BATCH]


def _mixer_out_kernel(x_ref, mod_ref, ohf_ref, ohb_ref, ogf_ref, ogb_ref, hgate_ref, ggate_ref,
                      hw_ref, gw_ref, wo_ref, o_ref, cat_ref):
    def norm_gate(of_ref, ob_ref, gate_ref, w_ref, base):
        for h in range(HEADS):
            hc = slice(h * HEAD_DIM, (h + 1) * HEAD_DIM)
            o = of_ref[:, hc] + ob_ref[:, hc]
            on = o * lax.rsqrt(jnp.mean(o * o, axis=-1, keepdims=True) + EPS) * w_ref[...]
            gate = gate_ref[:, hc]
            cat_ref[:, base + h * HEAD_DIM:base + (h + 1) * HEAD_DIM] = (on * (gate * _sigmoid(gate))).astype(BF16)

    norm_gate(ohf_ref, ohb_ref, hgate_ref, hw_ref, 0)
    norm_gate(ogf_ref, ogb_ref, ggate_ref, gw_ref, GROUP_W)
    o_ref[...] = x_ref[...] + mod_ref[0, 5:6, :] * _dot(cat_ref[...], wo_ref[...])


def _mixer_out(x, mod, o_hg_f, o_hg_b, o_gd_f, o_gd_b, proj, hg_norm_w, gd_norm_w, w_out):
    tok = lambda col: pl.BlockSpec((OUT_TM, GROUP_W), lambda i: (i, col))
    vec = pl.BlockSpec((1, HEAD_DIM), lambda i: (0, 0))
    return pl.pallas_call(
        _mixer_out_kernel,
        grid=(N_TOK // OUT_TM,),
        in_specs=[
            pl.BlockSpec((OUT_TM, D_MODEL), lambda i: (i, 0)),
            pl.BlockSpec((1, N_MOD, D_MODEL), lambda i: (_mod_row(i, OUT_TM), 0, 0)),
            tok(0), tok(0), tok(0), tok(0), tok(4), tok(8),
            vec, vec,
            pl.BlockSpec((D_MODEL, D_MODEL), lambda i: (0, 0)),
        ],
        out_specs=pl.BlockSpec((OUT_TM, D_MODEL), lambda i: (i, 0)),
        out_shape=jax.ShapeDtypeStruct((N_TOK, D_MODEL), F32),
        scratch_shapes=[pltpu.VMEM((OUT_TM, D_MODEL), BF16)],
        name="mixer_out",
        compiler_params=pltpu.CompilerParams(
            dimension_semantics=("parallel",), vmem_limit_bytes=VMEM_LIMIT),
    )(x, mod, o_hg_f, o_hg_b, o_gd_f, o_gd_b, proj, proj,
      hg_norm_w.reshape(1, HEAD_DIM), gd_norm_w.reshape(1, HEAD_DIM), w_out)


def _final_norm_kernel(x_ref, w_ref, o_ref):
    x = x_ref[...]
    o_ref[...] = x * lax.rsqrt(jnp.mean(x * x, axis=-1, keepdims=True) + EPS) * w_ref[...]


def _final_norm(x, w, row0, n_rows):
    blk0 = row0 // NORM_TM
    return pl.pallas_call(
        _final_norm_kernel,
        grid=(n_rows // NORM_TM,),
        in_specs=[pl.BlockSpec((NORM_TM, D_MODEL), lambda i: (blk0 + i, 0)),
                  pl.BlockSpec((1, D_MODEL), lambda i: (0, 0))],
        out_specs=pl.BlockSpec((NORM_TM, D_MODEL), lambda i: (i, 0)),
        out_shape=jax.ShapeDtypeStruct((n_rows, D_MODEL), F32),
        name="final_norm",
        compiler_params=pltpu.CompilerParams(
            dimension_semantics=("parallel",), vmem_limit_bytes=VMEM_LIMIT),
    )(x, w.reshape(1, D_MODEL))


def _pad_cols(w, n):
    return jnp.pad(w, ((0, 0), (0, n - w.shape[1])))


def _lane_vec(v):
    return jnp.pad(v.reshape(-1).astype(F32), (0, TAIL_W - 2 * HEADS))


def kernel(x_prompt, x_sample, c, state_hgrn2, state_gdn, c_ctx, norm_w, w_mod, b_mod, ffn_w_in, ffn_w_out,
           w_in, hg_lower_bounds, hg_norm_w, gd_conv_w, gd_A_log, gd_dt_bias, gd_norm_w, w_out, final_norm_w):
    x = jnp.concatenate([x_prompt.reshape(N_PROMPT, D_MODEL), x_sample.reshape(N_SAMPLE, D_MODEL)], axis=0)
    cond = jnp.concatenate([c_ctx[None, :], c, jnp.zeros((N_COND - 1 - DEC_BATCH, D_MODEL), F32)], axis=0)
    mods = _modulation(cond, w_mod, b_mod)

    cs = jnp.cumsum(jax.nn.softmax(hg_lower_bounds.astype(F32), axis=1), axis=1)
    lbs = cs - cs[:, :1]

    def ffn_half(x, l, k):
        wg = _pad_cols(ffn_w_in[l, k][:, :D_FF].astype(BF16), D_FF_PAD)
        wu = _pad_cols(ffn_w_in[l, k][:, D_FF:].astype(BF16), D_FF_PAD)
        w2 = jnp.pad(ffn_w_out[l, k].astype(BF16), ((0, D_FF_PAD - D_FF), (0, 0)))
        return _ffn(x, mods[l], norm_w[l, 2 * k], wg, wu, w2, 6 * k)

    def mixer(x, l):
        w_main = w_in[l][:, :N_MAIN].astype(BF16)
        w_tail = _pad_cols(w_in[l][:, N_MAIN:], TAIL_W).astype(BF16)
        proj, tail = _mixer_in(x, mods[l], norm_w[l, 1], w_main, w_tail)
        tail_t = tail.reshape(N_TOK // GD_C, GD_C, TAIL_W).transpose(0, 2, 1)
        qn, kn, vc = _gdn_pre(proj, gd_conv_w[l])
        arow = _lane_vec(gd_A_log[l])
        drow = _lane_vec(gd_dt_bias[l])
        gd_args = (qn, kn, vc, tail, tail_t, arow.reshape(1, TAIL_W), drow.reshape(1, TAIL_W),
                   arow.reshape(TAIL_W, 1), drow.reshape(TAIL_W, 1))
        o_hg, o_gd, fin_hg, fin_gd = [], [], [], []
        for d in range(2):
            lb = lbs[d, l].reshape(1, GROUP_W)
            o, fin = _hgrn2_scan(proj, lb, jnp.log(lb), state_hgrn2, layer=l, reverse=d == 1)
            o_hg.append(o)
            fin_hg.append(fin)
            o, fin = _gdn_scan(*gd_args, state_gdn, layer=l, reverse=d == 1)
            o_gd.append(o)
            fin_gd.append(fin)
        x = _mixer_out(x, mods[l], o_hg[0], o_hg[1], o_gd[0], o_gd[1], proj,
                       hg_norm_w[l], gd_norm_w[l], w_out[l].astype(BF16))
        return x, jnp.stack(fin_hg, axis=1), jnp.stack(fin_gd, axis=1)

    new_hg, new_gd = [], []
    for l in range(DEPTH):
        x = ffn_half(x, l, 0)
        x, s_hg, s_gd = mixer(x, l)
        new_hg.append(s_hg)
        new_gd.append(s_gd)
        x = ffn_half(x, l, 1)

    y_prompt = _final_norm(x, final_norm_w, 0, N_PROMPT).reshape(BATCH, SEQ, D_MODEL)
    y_sample = _final_norm(x, final_norm_w, N_PROMPT, N_SAMPLE).reshape(DEC_BATCH, DEC_SEQ, D_MODEL)
    return (y_prompt, y_sample, jnp.stack(new_hg, axis=1), jnp.stack(new_gd, axis=1))
```

```python
import functools

import jax
import jax.numpy as jnp
from jax import lax
from jax.experimental import pallas as pl
from jax.experimental.pallas import tpu as pltpu

F32 = jnp.float32
BF16 = jnp.bfloat16

D_MODEL = 2048
BATCH = 32
SEQ = 256
DEPTH = 2
DEC_BATCH = 8
DEC_SEQ = 2048
GRID_W = 64
HEADS = 8
HEAD_DIM = 128
GROUP_W = HEADS * HEAD_DIM
CONV_K = 5
D_FF = 5504
N_MOD = 9
EPS = 1e-6

N_PROMPT = BATCH * SEQ
N_SAMPLE = DEC_BATCH * DEC_SEQ
N_TOK = N_PROMPT + N_SAMPLE
N_COND = 16
N_MAIN = 9 * GROUP_W
TAIL_W = 128

FF_TILE = 512
FFN_TM = 512
IN_TM = 1024
IN_TN = 1024
OUT_TM = 256
NORM_TM = 1024
MOD_TN = 1024
SCAN_TB = 256
HG_C = 32
HG_GROUP = 4
GD_C = 64
GD_TB = 256
EXP_CLAMP = 80.0
VMEM_LIMIT = 52 * 1024 * 1024


def _mod_row(i, tm):
    n_ctx = N_PROMPT // tm
    return jnp.where(i < n_ctx, 0, 1 + (i - n_ctx) // (DEC_SEQ // tm))


def _sigmoid(x):
    return jax.nn.sigmoid(x)


def _softplus(x):
    return jnp.maximum(x, 0.0) + jnp.log1p(jnp.exp(-jnp.abs(x)))


def _dot(a, b):
    return jnp.dot(a, b, preferred_element_type=F32)


def _dot_nt(a, b):
    return lax.dot_general(a, b, (((1,), (1,)), ((), ())), preferred_element_type=F32)


def _dot_tn(a, b):
    return lax.dot_general(a, b, (((0,), (0,)), ((), ())), preferred_element_type=F32)


def _split3_bf16(x):
    hi = x.astype(BF16)
    r1 = x - hi.astype(F32)
    mid = r1.astype(BF16)
    return hi, mid, (r1 - mid.astype(F32)).astype(BF16)


def _dot_exact_lhs(a, x):
    hi, mid, lo = _split3_bf16(x)
    return _dot(a, hi) + (_dot(a, mid) + _dot(a, lo))


def _dot_exact_rhs(x, a):
    hi, mid, lo = _split3_bf16(x)
    return _dot(hi, a) + (_dot(mid, a) + _dot(lo, a))


def _each(f, *lists):
    return [f(*args) for args in zip(*lists)]


def _norm_modulate(x, nw, scale, shift):
    ms = jnp.mean(x * x, axis=-1, keepdims=True)
    return x * lax.rsqrt(ms + EPS) * nw * (1.0 + scale) + shift


def _mod_kernel(c_ref, w_ref, b_ref, o_ref):
    c = c_ref[...]
    s = (c * _sigmoid(c)).astype(BF16)
    o_ref[0] = _dot(s, w_ref[0].astype(BF16)) + b_ref[0]


def _modulation(cond, w_mod, b_mod):
    n_out = N_MOD * D_MODEL
    out = pl.pallas_call(
        _mod_kernel,
        grid=(DEPTH, n_out // MOD_TN),
        in_specs=[
            pl.BlockSpec((N_COND, D_MODEL), lambda l, j: (0, 0)),
            pl.BlockSpec((1, D_MODEL, MOD_TN), lambda l, j: (l, 0, j)),
            pl.BlockSpec((1, 1, MOD_TN), lambda l, j: (l, 0, j)),
        ],
        out_specs=pl.BlockSpec((1, N_COND, MOD_TN), lambda l, j: (l, 0, j)),
        out_shape=jax.ShapeDtypeStruct((DEPTH, N_COND, n_out), F32),
        name="modulation",
        compiler_params=pltpu.CompilerParams(
            dimension_semantics=("arbitrary", "arbitrary"), vmem_limit_bytes=VMEM_LIMIT),
    )(cond, w_mod, b_mod.reshape(DEPTH, 1, n_out))
    return out.reshape(DEPTH, N_COND, N_MOD, D_MODEL)


def _ffn_kernel(x_ref, mod_ref, nw_ref, wg_ref, wu_ref, w2_ref, o_ref, hn_ref, acc_ref, *, i_mod, nj):
    j = pl.program_id(1)

    @pl.when(j == 0)
    def _():
        h = _norm_modulate(x_ref[...], nw_ref[...],
                           mod_ref[0, i_mod + 1:i_mod + 2, :], mod_ref[0, i_mod:i_mod + 1, :])
        hn_ref[...] = h.astype(BF16)
        acc_ref[...] = jnp.zeros_like(acc_ref)

    def accumulate(width):
        h = hn_ref[...]
        g = _dot(h, wg_ref[:, :width])
        u = _dot(h, wu_ref[:, :width])
        a = (g * _sigmoid(g) * u).astype(BF16)
        acc_ref[...] += _dot(a, w2_ref[:width, :])

    @pl.when(j < nj - 1)
    def _():
        accumulate(FF_TILE)

    @pl.when(j == nj - 1)
    def _():
        accumulate(D_FF - (nj - 1) * FF_TILE)
        o_ref[...] = x_ref[...] + 0.5 * mod_ref[0, i_mod + 2:i_mod + 3, :] * acc_ref[...]


def _ffn(x, mod, nw, wg, wu, w2, i_mod):
    nj = pl.cdiv(D_FF, FF_TILE)
    return pl.pallas_call(
        functools.partial(_ffn_kernel, i_mod=i_mod, nj=nj),
        grid=(N_TOK // FFN_TM, nj),
        in_specs=[
            pl.BlockSpec((FFN_TM, D_MODEL), lambda i, j: (i, 0)),
            pl.BlockSpec((1, N_MOD, D_MODEL), lambda i, j: (_mod_row(i, FFN_TM), 0, 0)),
            pl.BlockSpec((1, D_MODEL), lambda i, j: (0, 0)),
            pl.BlockSpec((D_MODEL, FF_TILE), lambda i, j: (0, j)),
            pl.BlockSpec((D_MODEL, FF_TILE), lambda i, j: (0, j)),
            pl.BlockSpec((FF_TILE, D_MODEL), lambda i, j: (j, 0)),
        ],
        out_specs=pl.BlockSpec((FFN_TM, D_MODEL), lambda i, j: (i, 0)),
        out_shape=jax.ShapeDtypeStruct((N_TOK, D_MODEL), F32),
        scratch_shapes=[pltpu.VMEM((FFN_TM, D_MODEL), BF16), pltpu.VMEM((FFN_TM, D_MODEL), F32)],
        name="ffn",
        compiler_params=pltpu.CompilerParams(
            dimension_semantics=("parallel", "arbitrary"), vmem_limit_bytes=VMEM_LIMIT),
    )(x, mod, nw.reshape(1, D_MODEL), wg, wu, w2)


def _mixer_in_kernel(x_ref, mod_ref, nw_ref, w_ref, wt_ref, o_ref, t_ref, hn_ref):
    j = pl.program_id(1)

    @pl.when(j == 0)
    def _():
        h = _norm_modulate(x_ref[...], nw_ref[...], mod_ref[0, 4:5, :], mod_ref[0, 3:4, :]).astype(BF16)
        hn_ref[...] = h
        t_ref[...] = _dot(h, wt_ref[...])

    o_ref[...] = _dot(hn_ref[...], w_ref[...])


def _mixer_in(x, mod, nw, w_in, w_tail, layer):
    return pl.pallas_call(
        _mixer_in_kernel,
        grid=(N_TOK // IN_TM, N_MAIN // IN_TN),
        in_specs=[
            pl.BlockSpec((IN_TM, D_MODEL), lambda i, j: (i, 0)),
            pl.BlockSpec((1, N_MOD, D_MODEL), lambda i, j: (_mod_row(i, IN_TM), 0, 0)),
            pl.BlockSpec((1, D_MODEL), lambda i, j: (0, 0)),
            pl.BlockSpec((None, D_MODEL, IN_TN), lambda i, j: (layer, 0, j)),
            pl.BlockSpec((D_MODEL, TAIL_W), lambda i, j: (0, 0)),
        ],
        out_specs=[
            pl.BlockSpec((IN_TM, IN_TN), lambda i, j: (i, j)),
            pl.BlockSpec((IN_TM, TAIL_W), lambda i, j: (i, 0)),
        ],
        out_shape=[jax.ShapeDtypeStruct((N_TOK, N_MAIN), F32),
                   jax.ShapeDtypeStruct((N_TOK, TAIL_W), F32)],
        scratch_shapes=[pltpu.VMEM((IN_TM, D_MODEL), BF16)],
        name="mixer_in",
        compiler_params=pltpu.CompilerParams(
            dimension_semantics=("parallel", "arbitrary"), vmem_limit_bytes=VMEM_LIMIT),
    )(x, mod, nw.reshape(1, D_MODEL), w_in, w_tail)


def _seq_pos(g, n_ctx_blk, ctx_len, lat_len, reverse):
    is_ctx = g < n_ctx_blk
    gl = jnp.maximum(g - n_ctx_blk, 0)
    seq = jnp.where(is_ctx, g // ctx_len, gl // lat_len)
    pos = jnp.where(is_ctx, g % ctx_len, gl % lat_len)
    n = jnp.where(is_ctx, ctx_len, lat_len)
    base = jnp.where(is_ctx, seq * ctx_len, n_ctx_blk + seq * lat_len)
    blk = base + ((n - 1 - pos) if reverse else pos)
    return is_ctx, seq, pos, n, blk


def _tri_mask(n, reverse):
    row = lax.broadcasted_iota(jnp.int32, (n, n), 0)
    col = lax.broadcasted_iota(jnp.int32, (n, n), 1)
    return (row <= col) if reverse else (row >= col)


def _hgrn2_kernel(q_ref, v_ref, f_ref, lb_ref, llb_ref, s0_ref, o_ref, fin_ref, st_ref, *, reverse):
    c = HG_C
    nch = SCAN_TB // c
    is_ctx, _, pos, seq_len, _ = _seq_pos(pl.program_id(0), N_PROMPT // SCAN_TB, SEQ // SCAN_TB,
                                          DEC_SEQ // SCAN_TB, reverse)

    @pl.when(pos == 0)
    def _():
        for h in range(HEADS):
            st_ref[h] = jnp.where(is_ctx, 0.0, s0_ref[0, 0, 0, h].T)

    incl = _tri_mask(c, reverse)
    tri = jnp.where(incl, 1.0, 0.0).astype(BF16)
    mid = c // 2
    hcs = [slice(h * HEAD_DIM, (h + 1) * HEAD_DIM) for h in range(HEADS)]

    lb = lb_ref[...]
    llb = llb_ref[...]
    l1mlb = jnp.log1p(-lb)
    omlb = 1.0 - lb

    def gates(rows):
        fr = f_ref[rows, :]
        qr = q_ref[rows, :]
        e = jnp.exp(-jnp.abs(fr))
        d = 1.0 + e
        b = l1mlb + jnp.minimum(fr, 0.0) - jnp.log(d)
        m = jnp.maximum(llb, b)
        logf = m + jnp.log(jnp.exp(llb - m) + jnp.exp(b - m))
        k = omlb * jnp.where(fr >= 0.0, e, 1.0) / d
        q = qr * _sigmoid(qr) * (HEAD_DIM ** -0.5)
        return logf, k, q

    def scaled(logf, k, q, bc):
        btot = jnp.sum(logf, axis=0, keepdims=True)
        r = bc[mid:mid + 1, :]
        q0 = (q * jnp.exp(bc)).astype(BF16)
        qm = (q * jnp.exp(jnp.minimum(bc - r, EXP_CLAMP))).astype(BF16)
        km = (k * jnp.exp(jnp.minimum(r - bc, EXP_CLAMP))).astype(BF16)
        kl = (k * jnp.exp(btot - bc)).astype(BF16)
        return q0, qm, km, kl, jnp.exp(btot)

    order = list(reversed(range(nch))) if reverse else list(range(nch))
    st = [st_ref[h] for h in range(HEADS)]
    for g0 in range(0, nch, HG_GROUP):
        rows = [slice(nn * c, (nn + 1) * c) for nn in order[g0:g0 + HG_GROUP]]
        gate = [gates(r) for r in rows]
        bc = [_dot_exact_lhs(tri, gt[0]) for gt in gate]
        sc = [scaled(*gt, b_) for gt, b_ in zip(gate, bc)]
        vb = [v_ref[r, :].astype(BF16) for r in rows]
        att = [[jnp.where(incl, _dot_nt(s_[1][:, hc], s_[2][:, hc]), 0.0).astype(BF16) for hc in hcs] for s_ in sc]
        intra = [[_dot(a_[h], v_[:, hcs[h]]) for h in range(HEADS)] for a_, v_ in zip(att, vb)]
        upd = [[_dot_tn(v_[:, hc], s_[3][:, hc]) for hc in hcs] for s_, v_ in zip(sc, vb)]
        for i, r in enumerate(rows):
            q0, dec = sc[i][0], sc[i][4]
            inter = [_dot_nt(q0[:, hcs[h]], st[h].astype(BF16)) for h in range(HEADS)]
            for h in range(HEADS):
                o_ref[r, hcs[h]] = intra[i][h] + inter[h]
            st = [st[h] * dec[:, hcs[h]] + upd[i][h] for h in range(HEADS)]
    for h in range(HEADS):
        st_ref[h] = st[h]

    @pl.when(pos == seq_len - 1)
    def _():
        for h in range(HEADS):
            fin_ref[0, h] = st_ref[h].T


def _hgrn2_scan(proj, lb, llb, state, *, layer, reverse):
    d = 1 if reverse else 0
    n_ctx_blk, ctx_len, lat_len = N_PROMPT // SCAN_TB, SEQ // SCAN_TB, DEC_SEQ // SCAN_TB
    sp = lambda g: _seq_pos(g, n_ctx_blk, ctx_len, lat_len, reverse)
    tok = lambda col: pl.BlockSpec((SCAN_TB, GROUP_W), lambda g: (sp(g)[4], col))
    vec = pl.BlockSpec((1, GROUP_W), lambda g: (0, 0))
    st_blk = (1, HEADS, HEAD_DIM, HEAD_DIM)
    o, fin = pl.pallas_call(
        functools.partial(_hgrn2_kernel, reverse=reverse),
        name="hgrn2_scan",
        grid=(N_TOK // SCAN_TB,),
        in_specs=[tok(0), tok(1), tok(2 + d), vec, vec,
                  pl.BlockSpec((1, 1) + st_blk, lambda g: (jnp.where(sp(g)[0], 0, sp(g)[1]), layer, d, 0, 0, 0))],
        out_specs=[tok(0),
                   pl.BlockSpec(st_blk, lambda g: (jnp.where(sp(g)[0], sp(g)[1], BATCH), 0, 0, 0))],
        out_shape=[jax.ShapeDtypeStruct((N_TOK, GROUP_W), F32),
                   jax.ShapeDtypeStruct((BATCH + 1,) + st_blk[1:], F32)],
        scratch_shapes=[pltpu.VMEM((HEADS, HEAD_DIM, HEAD_DIM), F32)],
        compiler_params=pltpu.CompilerParams(dimension_semantics=("arbitrary",), vmem_limit_bytes=VMEM_LIMIT),
    )(proj, proj, proj, lb, llb, state)
    return o, fin[:BATCH]


def _gdn_pre_kernel(q_ref, k_ref, v_ref, cwq_ref, cwk_ref, cwv_ref, qo_ref, ko_ref, vo_ref):
    i = pl.program_id(0)
    period = jnp.where(i < N_PROMPT // SCAN_TB, SCAN_TB, GRID_W)
    pos = lax.broadcasted_iota(jnp.int32, (SCAN_TB, GROUP_W), 0) & (period - 1)
    offs = [j - CONV_K // 2 for j in range(CONV_K)]
    valid = [(pos + off >= 0) & (pos + off < period) for off in offs]

    def conv_silu(x_ref, cw_ref):
        x = x_ref[...]
        acc = x * cw_ref[CONV_K // 2:CONV_K // 2 + 1, :]
        for j, off in enumerate(offs):
            if off == 0:
                continue
            xs = pltpu.roll(x, (-off) % SCAN_TB, 0)
            acc = acc + jnp.where(valid[j], xs, 0.0) * cw_ref[j:j + 1, :]
        return acc * _sigmoid(acc)

    def l2norm_heads(y, o_ref, scale):
        for h in range(HEADS):
            hc = slice(h * HEAD_DIM, (h + 1) * HEAD_DIM)
            yh = y[:, hc]
            o_ref[:, hc] = yh * (lax.rsqrt(jnp.sum(yh * yh, axis=-1, keepdims=True) + EPS) * scale)

    l2norm_heads(conv_silu(q_ref, cwq_ref), qo_ref, HEAD_DIM ** -0.5)
    l2norm_heads(conv_silu(k_ref, cwk_ref), ko_ref, 1.0)
    vo_ref[...] = conv_silu(v_ref, cwv_ref)


def _gdn_pre(proj, conv_w):
    cw = [conv_w[:, i * GROUP_W:(i + 1) * GROUP_W] for i in range(3)]
    tok = lambda col: pl.BlockSpec((SCAN_TB, GROUP_W), lambda i: (i, col))
    cws = pl.BlockSpec((CONV_K, GROUP_W), lambda i: (0, 0))
    out = jax.ShapeDtypeStruct((N_TOK, GROUP_W), F32)
    return pl.pallas_call(
        _gdn_pre_kernel,
        grid=(N_TOK // SCAN_TB,),
        in_specs=[tok(5), tok(6), tok(7), cws, cws, cws],
        out_specs=[tok(0), tok(0), tok(0)],
        out_shape=[out, out, out],
        name="gdn_pre",
        compiler_params=pltpu.CompilerParams(
            dimension_semantics=("parallel",), vmem_limit_bytes=VMEM_LIMIT),
    )(proj, proj, proj, *cw)


def _gdn_kernel(q_ref, k_ref, v_ref, ab_ref, abt_ref, arow_ref, drow_ref, acol_ref, dcol_ref, s0_ref,
                o_ref, fin_ref, s_ref, *, reverse, d):
    c = GD_C
    ncs = GD_TB // c
    is_ctx, _, pos, seq_len, _ = _seq_pos(pl.program_id(0), N_PROMPT // GD_TB, SEQ // GD_TB,
                                          DEC_SEQ // GD_TB, reverse)

    @pl.when(pos == 0)
    def _():
        for h in range(HEADS):
            s_ref[h] = jnp.where(is_ctx, 0.0, s0_ref[0, 0, 0, h])

    incl = _tri_mask(c, reverse)
    incl_t = _tri_mask(c, not reverse)
    row = lax.broadcasted_iota(jnp.int32, (c, c), 0)
    col = lax.broadcasted_iota(jnp.int32, (c, c), 1)
    strict = incl & (row != col)
    eye = (row == col).astype(F32)
    same_blk = {lvl: (row >> lvl) == (col >> lvl) for lvl in range(3, c.bit_length())}
    tri = jnp.where(incl, 1.0, 0.0).astype(BF16)
    tri_t = jnp.where(incl_t, 1.0, 0.0).astype(BF16)
    last = 0 if reverse else c - 1
    rows = [slice(ci * c, (ci + 1) * c) for ci in range(ncs)]

    gcum, gcum_t, beta_tok = [], [], []
    for ci in range(ncs):
        ab = ab_ref[rows[ci], :]
        gcum.append(_dot_exact_lhs(tri, -jnp.exp(arow_ref[...]) * _softplus(ab + drow_ref[...])))
        g_head = -jnp.exp(acol_ref[...]) * _softplus(abt_ref[ci] + dcol_ref[...])
        gcum_t.append(_dot_exact_rhs(g_head, tri_t))
        beta_tok.append(_sigmoid(ab))

    hcs = [slice(h * HEAD_DIM, (h + 1) * HEAD_DIM) for h in range(HEADS)]
    prob = [(ci, h) for ci in range(ncs) for h in range(HEADS)]
    ia = [d * HEADS + h for h in range(HEADS)]
    ib = [2 * HEADS + d * HEADS + h for h in range(HEADS)]
    bf = lambda xs: [x.astype(BF16) for x in xs]
    qn = [q_ref[rows[ci], hcs[h]] for ci, h in prob]
    kn = [k_ref[rows[ci], hcs[h]] for ci, h in prob]
    vv = [v_ref[rows[ci], hcs[h]] for ci, h in prob]
    gcol = [gcum[ci][:, ia[h]:ia[h] + 1] for ci, h in prob]
    grow = [gcum_t[ci][ia[h]:ia[h] + 1, :] for ci, h in prob]
    bcol = [beta_tok[ci][:, ib[h]:ib[h] + 1] for ci, h in prob]
    glast = [gcum[ci][last:last + 1, ia[h]:ia[h] + 1] for ci, h in prob]
    decay = _each(lambda gc, gr: jnp.exp(jnp.where(incl, gc - gr, -1e30)), gcol, grow)
    knb = bf(kn)
    kb = _each(jnp.multiply, kn, bcol)
    kk = _each(_dot_nt, bf(kb), knb)
    a = _each(lambda x, dc: jnp.where(strict, x * dc, 0.0), kk, decay)
    bdot = lambda xs, ys: _each(_dot, bf(xs), bf(ys))
    p = [jnp.where(same_blk[3], -x, 0.0) for x in a]
    t = [eye + x for x in p]
    p = bdot(p, p)
    t = _each(jnp.add, t, bdot(t, p))
    p = bdot(p, p)
    t = _each(jnp.add, t, bdot(t, p))
    for lvl in range(3, GD_C.bit_length() - 1):
        couple = same_blk[lvl + 1] & jnp.logical_not(same_blk[lvl])
        n = [jnp.where(couple, x, 0.0) for x in a]
        t = _each(jnp.subtract, t, bdot(bdot(t, n), t))
    rhs = _each(lambda v_, b_, kb_, gc: jnp.concatenate([v_ * b_, kb_ * jnp.exp(gc)], axis=-1), vv, bcol, kb, gcol)
    sol = bdot(t, rhs)
    qk = _each(_dot_nt, bf(qn), knb)
    aqk = bf(_each(jnp.multiply, qk, decay))
    u = [x[:, :HEAD_DIM] for x in sol]
    wb = bf([x[:, HEAD_DIM:] for x in sol])
    qe = bf(_each(lambda q_, gc: q_ * jnp.exp(gc), qn, gcol))
    kdec = bf(_each(lambda k_, gl, gc: k_ * jnp.exp(gl - gc), kn, glast, gcol))
    sdec = [jnp.exp(x) for x in glast]

    s = [s_ref[h] for h in range(HEADS)]
    for ci in (reversed(range(ncs)) if reverse else range(ncs)):
        idx = [ci * HEADS + h for h in range(HEADS)]
        sb = bf(s)
        ws = [_dot(wb[i], sb[h]) for h, i in enumerate(idx)]
        vnb = bf([u[i] - ws[h] for h, i in enumerate(idx)])
        qs = [_dot(qe[i], sb[h]) for h, i in enumerate(idx)]
        av = [_dot(aqk[i], vnb[h]) for h, i in enumerate(idx)]
        kv = [_dot_tn(kdec[i], vnb[h]) for h, i in enumerate(idx)]
        for h, i in enumerate(idx):
            o_ref[rows[ci], hcs[h]] = qs[h] + av[h]
        s = [s[h] * sdec[i] + kv[h] for h, i in enumerate(idx)]
    for h in range(HEADS):
        s_ref[h] = s[h]

    @pl.when(pos == seq_len - 1)
    def _():
        for h in range(HEADS):
            fin_ref[0, h] = s_ref[h]


def _gdn_scan(qn, kn, vc, tail, tail_t, arow, drow, acol, dcol, state, *, layer, reverse):
    d = 1 if reverse else 0
    sp = lambda g: _seq_pos(g, N_PROMPT // GD_TB, SEQ // GD_TB, DEC_SEQ // GD_TB, reverse)
    tok = pl.BlockSpec((GD_TB, GROUP_W), lambda g: (sp(g)[4], 0))
    vec_row = pl.BlockSpec((1, TAIL_W), lambda g: (0, 0))
    vec_col = pl.BlockSpec((TAIL_W, 1), lambda g: (0, 0))
    st_blk = (1, HEADS, HEAD_DIM, HEAD_DIM)
    o, fin = pl.pallas_call(
        functools.partial(_gdn_kernel, reverse=reverse, d=d),
        name="gdn_scan",
        grid=(N_TOK // GD_TB,),
        in_specs=[tok, tok, tok,
                  pl.BlockSpec((GD_TB, TAIL_W), lambda g: (sp(g)[4], 0)),
                  pl.BlockSpec((GD_TB // GD_C, TAIL_W, GD_C), lambda g: (sp(g)[4], 0, 0)),
                  vec_row, vec_row, vec_col, vec_col,
                  pl.BlockSpec((1, 1) + st_blk, lambda g: (jnp.where(sp(g)[0], 0, sp(g)[1]), layer, d, 0, 0, 0))],
        out_specs=[tok,
                   pl.BlockSpec(st_blk, lambda g: (jnp.where(sp(g)[0], sp(g)[1], BATCH), 0, 0, 0))],
        out_shape=[jax.ShapeDtypeStruct((N_TOK, GROUP_W), F32),
                   jax.ShapeDtypeStruct((BATCH + 1,) + st_blk[1:], F32)],
        scratch_shapes=[pltpu.VMEM((HEADS, HEAD_DIM, HEAD_DIM), F32)],
        compiler_params=pltpu.CompilerParams(dimension_semantics=("arbitrary",), vmem_limit_bytes=VMEM_LIMIT),
    )(qn, kn, vc, tail, tail_t, arow, drow, acol, dcol, state)
    return o, fin[:BATCH]


def _mixer_out_kernel(x_ref, mod_ref, ohf_ref, ohb_ref, ogf_ref, ogb_ref, hgate_ref, ggate_ref,
                      hw_ref, gw_ref, wo_ref, o_ref, cat_ref):
    def norm_gate(of_ref, ob_ref, gate_ref, w_ref, base):
        for h in range(HEADS):
            hc = slice(h * HEAD_DIM, (h + 1) * HEAD_DIM)
            o = of_ref[:, hc] + ob_ref[:, hc]
            on = o * lax.rsqrt(jnp.mean(o * o, axis=-1, keepdims=True) + EPS) * w_ref[...]
            gate = gate_ref[:, hc]
            cat_ref[:, base + h * HEAD_DIM:base + (h + 1) * HEAD_DIM] = (on * (gate * _sigmoid(gate))).astype(BF16)

    norm_gate(ohf_ref, ohb_ref, hgate_ref, hw_ref, 0)
    norm_gate(ogf_ref, ogb_ref, ggate_ref, gw_ref, GROUP_W)
    o_ref[...] = x_ref[...] + mod_ref[0, 5:6, :] * _dot(cat_ref[...], wo_ref[...])


def _mixer_out(x, mod, o_hg_f, o_hg_b, o_gd_f, o_gd_b, proj, hg_norm_w, gd_norm_w, w_out):
    tok = lambda col: pl.BlockSpec((OUT_TM, GROUP_W), lambda i: (i, col))
    vec = pl.BlockSpec((1, HEAD_DIM), lambda i: (0, 0))
    return pl.pallas_call(
        _mixer_out_kernel,
        grid=(N_TOK // OUT_TM,),
        in_specs=[
            pl.BlockSpec((OUT_TM, D_MODEL), lambda i: (i, 0)),
            pl.BlockSpec((1, N_MOD, D_MODEL), lambda i: (_mod_row(i, OUT_TM), 0, 0)),
            tok(0), tok(0), tok(0), tok(0), tok(4), tok(8),
            vec, vec,
            pl.BlockSpec((D_MODEL, D_MODEL), lambda i: (0, 0)),
        ],
        out_specs=pl.BlockSpec((OUT_TM, D_MODEL), lambda i: (i, 0)),
        out_shape=jax.ShapeDtypeStruct((N_TOK, D_MODEL), F32),
        scratch_shapes=[pltpu.VMEM((OUT_TM, D_MODEL), BF16)],
        name="mixer_out",
        compiler_params=pltpu.CompilerParams(
            dimension_semantics=("parallel",), vmem_limit_bytes=VMEM_LIMIT),
    )(x, mod, o_hg_f, o_hg_b, o_gd_f, o_gd_b, proj, proj,
      hg_norm_w.reshape(1, HEAD_DIM), gd_norm_w.reshape(1, HEAD_DIM), w_out)


def _final_norm_kernel(x_ref, w_ref, o_ref):
    x = x_ref[...]
    o_ref[...] = x * lax.rsqrt(jnp.mean(x * x, axis=-1, keepdims=True) + EPS) * w_ref[...]


def _final_norm(x, w, row0, n_rows):
    blk0 = row0 // NORM_TM
    return pl.pallas_call(
        _final_norm_kernel,
        grid=(n_rows // NORM_TM,),
        in_specs=[pl.BlockSpec((NORM_TM, D_MODEL), lambda i: (blk0 + i, 0)),
                  pl.BlockSpec((1, D_MODEL), lambda i: (0, 0))],
        out_specs=pl.BlockSpec((NORM_TM, D_MODEL), lambda i: (i, 0)),
        out_shape=jax.ShapeDtypeStruct((n_rows, D_MODEL), F32),
        name="final_norm",
        compiler_params=pltpu.CompilerParams(
            dimension_semantics=("parallel",), vmem_limit_bytes=VMEM_LIMIT),
    )(x, w.reshape(1, D_MODEL))


def _pad_cols(w, n):
    return jnp.pad(w, ((0, 0), (0, n - w.shape[1])))


def _lane_vec(v):
    return jnp.pad(v.reshape(-1).astype(F32), (0, TAIL_W - 2 * HEADS))


def kernel(x_prompt, x_sample, c, state_hgrn2, state_gdn, c_ctx, norm_w, w_mod, b_mod, ffn_w_in, ffn_w_out,
           w_in, hg_lower_bounds, hg_norm_w, gd_conv_w, gd_A_log, gd_dt_bias, gd_norm_w, w_out, final_norm_w):
    x = jnp.concatenate([x_prompt.reshape(N_PROMPT, D_MODEL), x_sample.reshape(N_SAMPLE, D_MODEL)], axis=0)
    cond = jnp.concatenate([c_ctx[None, :], c, jnp.zeros((N_COND - 1 - DEC_BATCH, D_MODEL), F32)], axis=0)
    mods = _modulation(cond, w_mod, b_mod)

    cs = jnp.cumsum(jax.nn.softmax(hg_lower_bounds.astype(F32), axis=1), axis=1)
    lbs = cs - cs[:, :1]
    w_in_bf = w_in.astype(BF16)

    def ffn_half(x, l, k):
        wg = ffn_w_in[l, k][:, :D_FF].astype(BF16)
        wu = ffn_w_in[l, k][:, D_FF:].astype(BF16)
        return _ffn(x, mods[l], norm_w[l, 2 * k], wg, wu, ffn_w_out[l, k].astype(BF16), 6 * k)

    def mixer(x, l):
        w_tail = _pad_cols(w_in[l][:, N_MAIN:], TAIL_W).astype(BF16)
        proj, tail = _mixer_in(x, mods[l], norm_w[l, 1], w_in_bf, w_tail, l)
        tail_t = tail.reshape(N_TOK // GD_C, GD_C, TAIL_W).transpose(0, 2, 1)
        qn, kn, vc = _gdn_pre(proj, gd_conv_w[l])
        arow = _lane_vec(gd_A_log[l])
        drow = _lane_vec(gd_dt_bias[l])
        gd_args = (qn, kn, vc, tail, tail_t, arow.reshape(1, TAIL_W), drow.reshape(1, TAIL_W),
                   arow.reshape(TAIL_W, 1), drow.reshape(TAIL_W, 1))
        o_hg, o_gd, fin_hg, fin_gd = [], [], [], []
        for d in range(2):
            lb = lbs[d, l].reshape(1, GROUP_W)
            o, fin = _hgrn2_scan(proj, lb, jnp.log(lb), state_hgrn2, layer=l, reverse=d == 1)
            o_hg.append(o)
            fin_hg.append(fin)
            o, fin = _gdn_scan(*gd_args, state_gdn, layer=l, reverse=d == 1)
            o_gd.append(o)
            fin_gd.append(fin)
        x = _mixer_out(x, mods[l], o_hg[0], o_hg[1], o_gd[0], o_gd[1], proj,
                       hg_norm_w[l], gd_norm_w[l], w_out[l].astype(BF16))
        return x, jnp.stack(fin_hg, axis=1), jnp.stack(fin_gd, axis=1)

    new_hg, new_gd = [], []
    for l in range(DEPTH):
        x = ffn_half(x, l, 0)
        x, s_hg, s_gd = mixer(x, l)
        new_hg.append(s_hg)
        new_gd.append(s_gd)
        x = ffn_half(x, l, 1)

    y_prompt = _final_norm(x, final_norm_w, 0, N_PROMPT).reshape(BATCH, SEQ, D_MODEL)
    y_sample = _final_norm(x, final_norm_w, N_PROMPT, N_SAMPLE).reshape(DEC_BATCH, DEC_SEQ, D_MODEL)
    return (y_prompt, y_sample, jnp.stack(new_hg, axis=1), jnp.stack(new_gd, axis=1))
```

```python
import functools

import jax
import jax.numpy as jnp
from jax import lax
from jax.experimental import pallas as pl
from jax.experimental.pallas import tpu as pltpu

F32 = jnp.float32
BF16 = jnp.bfloat16

D_MODEL = 2048
BATCH = 32
SEQ = 256
DEPTH = 2
DEC_BATCH = 8
DEC_SEQ = 2048
GRID_W = 64
HEADS = 8
HEAD_DIM = 128
GROUP_W = HEADS * HEAD_DIM
CONV_K = 5
D_FF = 5504
N_MOD = 9
EPS = 1e-6

N_PROMPT = BATCH * SEQ
N_SAMPLE = DEC_BATCH * DEC_SEQ
N_TOK = N_PROMPT + N_SAMPLE
N_COND = 16
N_MAIN = 9 * GROUP_W
TAIL_W = 128

FF_TILE = 512
FFN_TM = 512
FFN_ROWS = 256
IN_TM = 1024
IN_ROWS = 256
IN_TN = 1024
OUT_TM = 256
NORM_TM = 1024
MOD_TN = 1024
SCAN_TB = 256
HG_C = 32
HG_GROUP = 4
GD_C = 64
GD_TB = 256
EXP_CLAMP = 80.0
VMEM_LIMIT = 52 * 1024 * 1024


def _mod_row(i, tm):
    n_ctx = N_PROMPT // tm
    return jnp.where(i < n_ctx, 0, 1 + (i - n_ctx) // (DEC_SEQ // tm))


def _sigmoid(x):
    return jax.nn.sigmoid(x)


def _softplus(x):
    return jnp.maximum(x, 0.0) + jnp.log1p(jnp.exp(-jnp.abs(x)))


def _dot(a, b):
    return jnp.dot(a, b, preferred_element_type=F32)


def _dot_nt(a, b):
    return lax.dot_general(a, b, (((1,), (1,)), ((), ())), preferred_element_type=F32)


def _dot_tn(a, b):
    return lax.dot_general(a, b, (((0,), (0,)), ((), ())), preferred_element_type=F32)


def _split3_bf16(x):
    hi = x.astype(BF16)
    r1 = x - hi.astype(F32)
    mid = r1.astype(BF16)
    return hi, mid, (r1 - mid.astype(F32)).astype(BF16)


def _dot_exact_lhs(a, x):
    hi, mid, lo = _split3_bf16(x)
    return _dot(a, hi) + (_dot(a, mid) + _dot(a, lo))


def _dot_exact_rhs(x, a):
    hi, mid, lo = _split3_bf16(x)
    return _dot(hi, a) + (_dot(mid, a) + _dot(lo, a))


def _each(f, *lists):
    return [f(*args) for args in zip(*lists)]


def _norm_modulate(x, nw, scale, shift):
    ms = jnp.mean(x * x, axis=-1, keepdims=True)
    return x * lax.rsqrt(ms + EPS) * nw * (1.0 + scale) + shift


def _mod_kernel(c_ref, w_ref, b_ref, o_ref):
    c = c_ref[...]
    s = (c * _sigmoid(c)).astype(BF16)
    o_ref[0] = _dot(s, w_ref[0].astype(BF16)) + b_ref[0]


def _modulation(cond, w_mod, b_mod):
    n_out = N_MOD * D_MODEL
    out = pl.pallas_call(
        _mod_kernel,
        grid=(DEPTH, n_out // MOD_TN),
        in_specs=[
            pl.BlockSpec((N_COND, D_MODEL), lambda l, j: (0, 0)),
            pl.BlockSpec((1, D_MODEL, MOD_TN), lambda l, j: (l, 0, j)),
            pl.BlockSpec((1, 1, MOD_TN), lambda l, j: (l, 0, j)),
        ],
        out_specs=pl.BlockSpec((1, N_COND, MOD_TN), lambda l, j: (l, 0, j)),
        out_shape=jax.ShapeDtypeStruct((DEPTH, N_COND, n_out), F32),
        name="modulation",
        compiler_params=pltpu.CompilerParams(
            dimension_semantics=("arbitrary", "arbitrary"), vmem_limit_bytes=VMEM_LIMIT),
    )(cond, w_mod, b_mod.reshape(DEPTH, 1, n_out))
    return out.reshape(DEPTH, N_COND, N_MOD, D_MODEL)


def _ffn_kernel(x_ref, mod_ref, nw_ref, wg_ref, wu_ref, w2_ref, o_ref, hn_ref, acc_ref, *, i_mod, nj):
    j = pl.program_id(1)
    rows = [slice(r * FFN_ROWS, (r + 1) * FFN_ROWS) for r in range(FFN_TM // FFN_ROWS)]

    def accumulate(width, first=False):
        g, u = [], []
        for r in rows:
            if first:
                h = _norm_modulate(x_ref[r, :], nw_ref[...], mod_ref[0, i_mod + 1:i_mod + 2, :],
                                   mod_ref[0, i_mod:i_mod + 1, :]).astype(BF16)
                hn_ref[r, :] = h
            else:
                h = hn_ref[r, :]
            g.append(_dot(h, wg_ref[:, :width]))
            u.append(_dot(h, wu_ref[:, :width]))
        a = [(g_ * _sigmoid(g_) * u_).astype(BF16) for g_, u_ in zip(g, u)]
        for r, a_ in zip(rows, a):
            d = _dot(a_, w2_ref[:width, :])
            if first:
                acc_ref[r, :] = d
            else:
                acc_ref[r, :] += d

    @pl.when(j == 0)
    def _():
        accumulate(FF_TILE, first=True)

    @pl.when((j > 0) & (j < nj - 1))
    def _():
        accumulate(FF_TILE)

    @pl.when(j == nj - 1)
    def _():
        accumulate(D_FF - (nj - 1) * FF_TILE)
        o_ref[...] = x_ref[...] + 0.5 * mod_ref[0, i_mod + 2:i_mod + 3, :] * acc_ref[...]


def _ffn(x, mod, nw, wg, wu, w2, i_mod):
    nj = pl.cdiv(D_FF, FF_TILE)
    return pl.pallas_call(
        functools.partial(_ffn_kernel, i_mod=i_mod, nj=nj),
        grid=(N_TOK // FFN_TM, nj),
        in_specs=[
            pl.BlockSpec((FFN_TM, D_MODEL), lambda i, j: (i, 0)),
            pl.BlockSpec((1, N_MOD, D_MODEL), lambda i, j: (_mod_row(i, FFN_TM), 0, 0)),
            pl.BlockSpec((1, D_MODEL), lambda i, j: (0, 0)),
            pl.BlockSpec((D_MODEL, FF_TILE), lambda i, j: (0, j)),
            pl.BlockSpec((D_MODEL, FF_TILE), lambda i, j: (0, j)),
            pl.BlockSpec((FF_TILE, D_MODEL), lambda i, j: (j, 0)),
        ],
        out_specs=pl.BlockSpec((FFN_TM, D_MODEL), lambda i, j: (i, 0)),
        out_shape=jax.ShapeDtypeStruct((N_TOK, D_MODEL), F32),
        scratch_shapes=[pltpu.VMEM((FFN_TM, D_MODEL), BF16), pltpu.VMEM((FFN_TM, D_MODEL), F32)],
        name="ffn",
        compiler_params=pltpu.CompilerParams(
            dimension_semantics=("parallel", "arbitrary"), vmem_limit_bytes=VMEM_LIMIT),
    )(x, mod, nw.reshape(1, D_MODEL), wg, wu, w2)


def _mixer_in_kernel(x_ref, mod_ref, nw_ref, w_ref, wt_ref, o_ref, t_ref, hn_ref):
    j = pl.program_id(1)

    @pl.when(j == 0)
    def _():
        for r0 in range(0, IN_TM, IN_ROWS):
            r = slice(r0, r0 + IN_ROWS)
            h = _norm_modulate(x_ref[r, :], nw_ref[...], mod_ref[0, 4:5, :], mod_ref[0, 3:4, :]).astype(BF16)
            hn_ref[r, :] = h
            o_ref[r, :] = _dot(h, w_ref[...])
            t_ref[r, :] = _dot(h, wt_ref[...])

    @pl.when(j > 0)
    def _():
        o_ref[...] = _dot(hn_ref[...], w_ref[...])


def _mixer_in(x, mod, nw, w_in, w_tail, layer):
    return pl.pallas_call(
        _mixer_in_kernel,
        grid=(N_TOK // IN_TM, N_MAIN // IN_TN),
        in_specs=[
            pl.BlockSpec((IN_TM, D_MODEL), lambda i, j: (i, 0)),
            pl.BlockSpec((1, N_MOD, D_MODEL), lambda i, j: (_mod_row(i, IN_TM), 0, 0)),
            pl.BlockSpec((1, D_MODEL), lambda i, j: (0, 0)),
            pl.BlockSpec((None, D_MODEL, IN_TN), lambda i, j: (layer, 0, j)),
            pl.BlockSpec((D_MODEL, TAIL_W), lambda i, j: (0, 0)),
        ],
        out_specs=[
            pl.BlockSpec((IN_TM, IN_TN), lambda i, j: (i, j)),
            pl.BlockSpec((IN_TM, TAIL_W), lambda i, j: (i, 0)),
        ],
        out_shape=[jax.ShapeDtypeStruct((N_TOK, N_MAIN), F32),
                   jax.ShapeDtypeStruct((N_TOK, TAIL_W), F32)],
        scratch_shapes=[pltpu.VMEM((IN_TM, D_MODEL), BF16)],
        name="mixer_in",
        compiler_params=pltpu.CompilerParams(
            dimension_semantics=("parallel", "arbitrary"), vmem_limit_bytes=VMEM_LIMIT),
    )(x, mod, nw.reshape(1, D_MODEL), w_in, w_tail)


def _seq_pos(g, n_ctx_blk, ctx_len, lat_len, reverse):
    is_ctx = g < n_ctx_blk
    gl = jnp.maximum(g - n_ctx_blk, 0)
    seq = jnp.where(is_ctx, g // ctx_len, gl // lat_len)
    pos = jnp.where(is_ctx, g % ctx_len, gl % lat_len)
    n = jnp.where(is_ctx, ctx_len, lat_len)
    base = jnp.where(is_ctx, seq * ctx_len, n_ctx_blk + seq * lat_len)
    blk = base + ((n - 1 - pos) if reverse else pos)
    return is_ctx, seq, pos, n, blk


def _tri_mask(n, reverse):
    row = lax.broadcasted_iota(jnp.int32, (n, n), 0)
    col = lax.broadcasted_iota(jnp.int32, (n, n), 1)
    return (row <= col) if reverse else (row >= col)


def _hgrn2_kernel(q_ref, v_ref, f_ref, lb_ref, llb_ref, s0_ref, o_ref, fin_ref, st_ref, *, reverse):
    c = HG_C
    nch = SCAN_TB // c
    is_ctx, _, pos, seq_len, _ = _seq_pos(pl.program_id(0), N_PROMPT // SCAN_TB, SEQ // SCAN_TB,
                                          DEC_SEQ // SCAN_TB, reverse)

    @pl.when(pos == 0)
    def _():
        for h in range(HEADS):
            st_ref[h] = jnp.where(is_ctx, 0.0, s0_ref[0, 0, 0, h].T)

    incl = _tri_mask(c, reverse)
    tri = jnp.where(incl, 1.0, 0.0).astype(BF16)
    mid = c // 2
    hcs = [slice(h * HEAD_DIM, (h + 1) * HEAD_DIM) for h in range(HEADS)]

    lb = lb_ref[...]
    llb = llb_ref[...]
    l1mlb = jnp.log1p(-lb)
    omlb = 1.0 - lb

    def gates(rows):
        fr = f_ref[rows, :]
        qr = q_ref[rows, :]
        e = jnp.exp(-jnp.abs(fr))
        d = 1.0 + e
        b = l1mlb + jnp.minimum(fr, 0.0) - jnp.log(d)
        m = jnp.maximum(llb, b)
        logf = m + jnp.log(jnp.exp(llb - m) + jnp.exp(b - m))
        k = omlb * jnp.where(fr >= 0.0, e, 1.0) / d
        q = qr * _sigmoid(qr) * (HEAD_DIM ** -0.5)
        return logf, k, q

    def scaled(logf, k, q, bc):
        btot = jnp.sum(logf, axis=0, keepdims=True)
        r = bc[mid:mid + 1, :]
        q0 = (q * jnp.exp(bc)).astype(BF16)
        qm = (q * jnp.exp(jnp.minimum(bc - r, EXP_CLAMP))).astype(BF16)
        km = (k * jnp.exp(jnp.minimum(r - bc, EXP_CLAMP))).astype(BF16)
        kl = (k * jnp.exp(btot - bc)).astype(BF16)
        return q0, qm, km, kl, jnp.exp(btot)

    order = list(reversed(range(nch))) if reverse else list(range(nch))
    st = [st_ref[h] for h in range(HEADS)]
    for g0 in range(0, nch, HG_GROUP):
        rows = [slice(nn * c, (nn + 1) * c) for nn in order[g0:g0 + HG_GROUP]]
        gate = [gates(r) for r in rows]
        bc = [_dot_exact_lhs(tri, gt[0]) for gt in gate]
        sc = [scaled(*gt, b_) for gt, b_ in zip(gate, bc)]
        vb = [v_ref[r, :].astype(BF16) for r in rows]
        att = [[jnp.where(incl, _dot_nt(s_[1][:, hc], s_[2][:, hc]), 0.0).astype(BF16) for hc in hcs] for s_ in sc]
        intra = [[_dot(a_[h], v_[:, hcs[h]]) for h in range(HEADS)] for a_, v_ in zip(att, vb)]
        upd = [[_dot_tn(v_[:, hc], s_[3][:, hc]) for hc in hcs] for s_, v_ in zip(sc, vb)]
        for i, r in enumerate(rows):
            q0, dec = sc[i][0], sc[i][4]
            inter = [_dot_nt(q0[:, hcs[h]], st[h].astype(BF16)) for h in range(HEADS)]
            for h in range(HEADS):
                o_ref[r, hcs[h]] = intra[i][h] + inter[h]
            st = [st[h] * dec[:, hcs[h]] + upd[i][h] for h in range(HEADS)]
    for h in range(HEADS):
        st_ref[h] = st[h]

    @pl.when(pos == seq_len - 1)
    def _():
        for h in range(HEADS):
            fin_ref[0, h] = st_ref[h].T


def _hgrn2_scan(proj, lb, llb, state, *, layer, reverse):
    d = 1 if reverse else 0
    n_ctx_blk, ctx_len, lat_len = N_PROMPT // SCAN_TB, SEQ // SCAN_TB, DEC_SEQ // SCAN_TB
    sp = lambda g: _seq_pos(g, n_ctx_blk, ctx_len, lat_len, reverse)
    tok = lambda col: pl.BlockSpec((SCAN_TB, GROUP_W), lambda g: (sp(g)[4], col))
    vec = pl.BlockSpec((1, GROUP_W), lambda g: (0, 0))
    st_blk = (1, HEADS, HEAD_DIM, HEAD_DIM)
    o, fin = pl.pallas_call(
        functools.partial(_hgrn2_kernel, reverse=reverse),
        name="hgrn2_scan",
        grid=(N_TOK // SCAN_TB,),
        in_specs=[tok(0), tok(1), tok(2 + d), vec, vec,
                  pl.BlockSpec((1, 1) + st_blk, lambda g: (jnp.where(sp(g)[0], 0, sp(g)[1]), layer, d, 0, 0, 0))],
        out_specs=[tok(0),
                   pl.BlockSpec(st_blk, lambda g: (jnp.where(sp(g)[0], sp(g)[1], BATCH), 0, 0, 0))],
        out_shape=[jax.ShapeDtypeStruct((N_TOK, GROUP_W), F32),
                   jax.ShapeDtypeStruct((BATCH + 1,) + st_blk[1:], F32)],
        scratch_shapes=[pltpu.VMEM((HEADS, HEAD_DIM, HEAD_DIM), F32)],
        compiler_params=pltpu.CompilerParams(dimension_semantics=("arbitrary",), vmem_limit_bytes=VMEM_LIMIT),
    )(proj, proj, proj, lb, llb, state)
    return o, fin[:BATCH]


def _gdn_pre_kernel(q_ref, k_ref, v_ref, cwq_ref, cwk_ref, cwv_ref, qo_ref, ko_ref, vo_ref):
    i = pl.program_id(0)
    period = jnp.where(i < N_PROMPT // SCAN_TB, SCAN_TB, GRID_W)
    pos = lax.broadcasted_iota(jnp.int32, (SCAN_TB, GROUP_W), 0) & (period - 1)
    offs = [j - CONV_K // 2 for j in range(CONV_K)]
    valid = [(pos + off >= 0) & (pos + off < period) for off in offs]

    def conv_silu(x_ref, cw_ref):
        x = x_ref[...]
        acc = x * cw_ref[CONV_K // 2:CONV_K // 2 + 1, :]
        for j, off in enumerate(offs):
            if off == 0:
                continue
            xs = pltpu.roll(x, (-off) % SCAN_TB, 0)
            acc = acc + jnp.where(valid[j], xs, 0.0) * cw_ref[j:j + 1, :]
        return acc * _sigmoid(acc)

    def l2norm_heads(y, o_ref, scale):
        for h in range(HEADS):
            hc = slice(h * HEAD_DIM, (h + 1) * HEAD_DIM)
            yh = y[:, hc]
            o_ref[:, hc] = yh * (lax.rsqrt(jnp.sum(yh * yh, axis=-1, keepdims=True) + EPS) * scale)

    l2norm_heads(conv_silu(q_ref, cwq_ref), qo_ref, HEAD_DIM ** -0.5)
    l2norm_heads(conv_silu(k_ref, cwk_ref), ko_ref, 1.0)
    vo_ref[...] = conv_silu(v_ref, cwv_ref)


def _gdn_pre(proj, conv_w):
    cw = [conv_w[:, i * GROUP_W:(i + 1) * GROUP_W] for i in range(3)]
    tok = lambda col: pl.BlockSpec((SCAN_TB, GROUP_W), lambda i: (i, col))
    cws = pl.BlockSpec((CONV_K, GROUP_W), lambda i: (0, 0))
    out = jax.ShapeDtypeStruct((N_TOK, GROUP_W), F32)
    return pl.pallas_call(
        _gdn_pre_kernel,
        grid=(N_TOK // SCAN_TB,),
        in_specs=[tok(5), tok(6), tok(7), cws, cws, cws],
        out_specs=[tok(0), tok(0), tok(0)],
        out_shape=[out, out, out],
        name="gdn_pre",
        compiler_params=pltpu.CompilerParams(
            dimension_semantics=("parallel",), vmem_limit_bytes=VMEM_LIMIT),
    )(proj, proj, proj, *cw)


def _gdn_kernel(q_ref, k_ref, v_ref, ab_ref, abt_ref, arow_ref, drow_ref, acol_ref, dcol_ref, s0_ref,
                o_ref, fin_ref, s_ref, *, reverse, d):
    c = GD_C
    ncs = GD_TB // c
    is_ctx, _, pos, seq_len, _ = _seq_pos(pl.program_id(0), N_PROMPT // GD_TB, SEQ // GD_TB,
                                          DEC_SEQ // GD_TB, reverse)

    @pl.when(pos == 0)
    def _():
        for h in range(HEADS):
            s_ref[h] = jnp.where(is_ctx, 0.0, s0_ref[0, 0, 0, h])

    incl = _tri_mask(c, reverse)
    incl_t = _tri_mask(c, not reverse)
    row = lax.broadcasted_iota(jnp.int32, (c, c), 0)
    col = lax.broadcasted_iota(jnp.int32, (c, c), 1)
    strict = incl & (row != col)
    eye = (row == col).astype(F32)
    same_blk = {lvl: (row >> lvl) == (col >> lvl) for lvl in range(3, c.bit_length())}
    tri = jnp.where(incl, 1.0, 0.0).astype(BF16)
    tri_t = jnp.where(incl_t, 1.0, 0.0).astype(BF16)
    last = 0 if reverse else c - 1
    rows = [slice(ci * c, (ci + 1) * c) for ci in range(ncs)]

    gcum, gcum_t, beta_tok = [], [], []
    for ci in range(ncs):
        ab = ab_ref[rows[ci], :]
        gcum.append(_dot_exact_lhs(tri, -jnp.exp(arow_ref[...]) * _softplus(ab + drow_ref[...])))
        g_head = -jnp.exp(acol_ref[...]) * _softplus(abt_ref[ci] + dcol_ref[...])
        gcum_t.append(_dot_exact_rhs(g_head, tri_t))
        beta_tok.append(_sigmoid(ab))

    hcs = [slice(h * HEAD_DIM, (h + 1) * HEAD_DIM) for h in range(HEADS)]
    prob = [(ci, h) for ci in range(ncs) for h in range(HEADS)]
    ia = [d * HEADS + h for h in range(HEADS)]
    ib = [2 * HEADS + d * HEADS + h for h in range(HEADS)]
    bf = lambda xs: [x.astype(BF16) for x in xs]
    qn = [q_ref[rows[ci], hcs[h]] for ci, h in prob]
    kn = [k_ref[rows[ci], hcs[h]] for ci, h in prob]
    vv = [v_ref[rows[ci], hcs[h]] for ci, h in prob]
    gcol = [gcum[ci][:, ia[h]:ia[h] + 1] for ci, h in prob]
    grow = [gcum_t[ci][ia[h]:ia[h] + 1, :] for ci, h in prob]
    bcol = [beta_tok[ci][:, ib[h]:ib[h] + 1] for ci, h in prob]
    glast = [gcum[ci][last:last + 1, ia[h]:ia[h] + 1] for ci, h in prob]
    decay = _each(lambda gc, gr: jnp.exp(jnp.where(incl, gc - gr, -1e30)), gcol, grow)
    knb = bf(kn)
    kb = _each(jnp.multiply, kn, bcol)
    kk = _each(_dot_nt, bf(kb), knb)
    a = _each(lambda x, dc: jnp.where(strict, x * dc, 0.0), kk, decay)
    bdot = lambda xs, ys: _each(_dot, bf(xs), bf(ys))
    p = [jnp.where(same_blk[3], -x, 0.0) for x in a]
    t = [eye + x for x in p]
    p = bdot(p, p)
    t = _each(jnp.add, t, bdot(t, p))
    p = bdot(p, p)
    t = _each(jnp.add, t, bdot(t, p))
    for lvl in range(3, GD_C.bit_length() - 1):
        couple = same_blk[lvl + 1] & jnp.logical_not(same_blk[lvl])
        n = [jnp.where(couple, x, 0.0) for x in a]
        t = _each(jnp.subtract, t, bdot(bdot(t, n), t))
    rhs = _each(lambda v_, b_, kb_, gc: jnp.concatenate([v_ * b_, kb_ * jnp.exp(gc)], axis=-1), vv, bcol, kb, gcol)
    sol = bdot(t, rhs)
    qk = _each(_dot_nt, bf(qn), knb)
    aqk = bf(_each(jnp.multiply, qk, decay))
    u = [x[:, :HEAD_DIM] for x in sol]
    wb = bf([x[:, HEAD_DIM:] for x in sol])
    qe = bf(_each(lambda q_, gc: q_ * jnp.exp(gc), qn, gcol))
    kdec = bf(_each(lambda k_, gl, gc: k_ * jnp.exp(gl - gc), kn, glast, gcol))
    sdec = [jnp.exp(x) for x in glast]

    s = [s_ref[h] for h in range(HEADS)]
    for ci in (reversed(range(ncs)) if reverse else range(ncs)):
        idx = [ci * HEADS + h for h in range(HEADS)]
        sb = bf(s)
        ws = [_dot(wb[i], sb[h]) for h, i in enumerate(idx)]
        vnb = bf([u[i] - ws[h] for h, i in enumerate(idx)])
        qs = [_dot(qe[i], sb[h]) for h, i in enumerate(idx)]
        av = [_dot(aqk[i], vnb[h]) for h, i in enumerate(idx)]
        kv = [_dot_tn(kdec[i], vnb[h]) for h, i in enumerate(idx)]
        for h, i in enumerate(idx):
            o_ref[rows[ci], hcs[h]] = qs[h] + av[h]
        s = [s[h] * sdec[i] + kv[h] for h, i in enumerate(idx)]
    for h in range(HEADS):
        s_ref[h] = s[h]

    @pl.when(pos == seq_len - 1)
    def _():
        for h in range(HEADS):
            fin_ref[0, h] = s_ref[h]


def _gdn_scan(qn, kn, vc, tail, tail_t, arow, drow, acol, dcol, state, *, layer, reverse):
    d = 1 if reverse else 0
    sp = lambda g: _seq_pos(g, N_PROMPT // GD_TB, SEQ // GD_TB, DEC_SEQ // GD_TB, reverse)
    tok = pl.BlockSpec((GD_TB, GROUP_W), lambda g: (sp(g)[4], 0))
    vec_row = pl.BlockSpec((1, TAIL_W), lambda g: (0, 0))
    vec_col = pl.BlockSpec((TAIL_W, 1), lambda g: (0, 0))
    st_blk = (1, HEADS, HEAD_DIM, HEAD_DIM)
    o, fin = pl.pallas_call(
        functools.partial(_gdn_kernel, reverse=reverse, d=d),
        name="gdn_scan",
        grid=(N_TOK // GD_TB,),
        in_specs=[tok, tok, tok,
                  pl.BlockSpec((GD_TB, TAIL_W), lambda g: (sp(g)[4], 0)),
                  pl.BlockSpec((GD_TB // GD_C, TAIL_W, GD_C), lambda g: (sp(g)[4], 0, 0)),
                  vec_row, vec_row, vec_col, vec_col,
                  pl.BlockSpec((1, 1) + st_blk, lambda g: (jnp.where(sp(g)[0], 0, sp(g)[1]), layer, d, 0, 0, 0))],
        out_specs=[tok,
                   pl.BlockSpec(st_blk, lambda g: (jnp.where(sp(g)[0], sp(g)[1], BATCH), 0, 0, 0))],
        out_shape=[jax.ShapeDtypeStruct((N_TOK, GROUP_W), F32),
                   jax.ShapeDtypeStruct((BATCH + 1,) + st_blk[1:], F32)],
        scratch_shapes=[pltpu.VMEM((HEADS, HEAD_DIM, HEAD_DIM), F32)],
        compiler_params=pltpu.CompilerParams(dimension_semantics=("arbitrary",), vmem_limit_bytes=VMEM_LIMIT),
    )(qn, kn, vc, tail, tail_t, arow, drow, acol, dcol, state)
    return o, fin[:BATCH]


def _mixer_out_kernel(x_ref, mod_ref, ohf_ref, ohb_ref, ogf_ref, ogb_ref, hgate_ref, ggate_ref,
                      hw_ref, gw_ref, wo_ref, o_ref, cat_ref):
    def norm_gate(of_ref, ob_ref, gate_ref, w_ref, base):
        for h in range(HEADS):
            hc = slice(h * HEAD_DIM, (h + 1) * HEAD_DIM)
            o = of_ref[:, hc] + ob_ref[:, hc]
            on = o * lax.rsqrt(jnp.mean(o * o, axis=-1, keepdims=True) + EPS) * w_ref[...]
            gate = gate_ref[:, hc]
            cat_ref[:, base + h * HEAD_DIM:base + (h + 1) * HEAD_DIM] = (on * (gate * _sigmoid(gate))).astype(BF16)

    norm_gate(ohf_ref, ohb_ref, hgate_ref, hw_ref, 0)
    norm_gate(ogf_ref, ogb_ref, ggate_ref, gw_ref, GROUP_W)
    o_ref[...] = x_ref[...] + mod_ref[0, 5:6, :] * _dot(cat_ref[...], wo_ref[...])


def _mixer_out(x, mod, o_hg_f, o_hg_b, o_gd_f, o_gd_b, proj, hg_norm_w, gd_norm_w, w_out):
    tok = lambda col: pl.BlockSpec((OUT_TM, GROUP_W), lambda i: (i, col))
    vec = pl.BlockSpec((1, HEAD_DIM), lambda i: (0, 0))
    return pl.pallas_call(
        _mixer_out_kernel,
        grid=(N_TOK // OUT_TM,),
        in_specs=[
            pl.BlockSpec((OUT_TM, D_MODEL), lambda i: (i, 0)),
            pl.BlockSpec((1, N_MOD, D_MODEL), lambda i: (_mod_row(i, OUT_TM), 0, 0)),
            tok(0), tok(0), tok(0), tok(0), tok(4), tok(8),
            vec, vec,
            pl.BlockSpec((D_MODEL, D_MODEL), lambda i: (0, 0)),
        ],
        out_specs=pl.BlockSpec((OUT_TM, D_MODEL), lambda i: (i, 0)),
        out_shape=jax.ShapeDtypeStruct((N_TOK, D_MODEL), F32),
        scratch_shapes=[pltpu.VMEM((OUT_TM, D_MODEL), BF16)],
        name="mixer_out",
        compiler_params=pltpu.CompilerParams(
            dimension_semantics=("parallel",), vmem_limit_bytes=VMEM_LIMIT),
    )(x, mod, o_hg_f, o_hg_b, o_gd_f, o_gd_b, proj, proj,
      hg_norm_w.reshape(1, HEAD_DIM), gd_norm_w.reshape(1, HEAD_DIM), w_out)


def _final_norm_kernel(x_ref, w_ref, o_ref):
    x = x_ref[...]
    o_ref[...] = x * lax.rsqrt(jnp.mean(x * x, axis=-1, keepdims=True) + EPS) * w_ref[...]


def _final_norm(x, w, row0, n_rows):
    blk0 = row0 // NORM_TM
    return pl.pallas_call(
        _final_norm_kernel,
        grid=(n_rows // NORM_TM,),
        in_specs=[pl.BlockSpec((NORM_TM, D_MODEL), lambda i: (blk0 + i, 0)),
                  pl.BlockSpec((1, D_MODEL), lambda i: (0, 0))],
        out_specs=pl.BlockSpec((NORM_TM, D_MODEL), lambda i: (i, 0)),
        out_shape=jax.ShapeDtypeStruct((n_rows, D_MODEL), F32),
        name="final_norm",
        compiler_params=pltpu.CompilerParams(
            dimension_semantics=("parallel",), vmem_limit_bytes=VMEM_LIMIT),
    )(x, w.reshape(1, D_MODEL))


def _pad_cols(w, n):
    return jnp.pad(w, ((0, 0), (0, n - w.shape[1])))


def _lane_vec(v):
    return jnp.pad(v.reshape(-1).astype(F32), (0, TAIL_W - 2 * HEADS))


def kernel(x_prompt, x_sample, c, state_hgrn2, state_gdn, c_ctx, norm_w, w_mod, b_mod, ffn_w_in, ffn_w_out,
           w_in, hg_lower_bounds, hg_norm_w, gd_conv_w, gd_A_log, gd_dt_bias, gd_norm_w, w_out, final_norm_w):
    x = jnp.concatenate([x_prompt.reshape(N_PROMPT, D_MODEL), x_sample.reshape(N_SAMPLE, D_MODEL)], axis=0)
    cond = jnp.concatenate([c_ctx[None, :], c, jnp.zeros((N_COND - 1 - DEC_BATCH, D_MODEL), F32)], axis=0)
    mods = _modulation(cond, w_mod, b_mod)

    cs = jnp.cumsum(jax.nn.softmax(hg_lower_bounds.astype(F32), axis=1), axis=1)
    lbs = cs - cs[:, :1]
    w_in_bf = w_in.astype(BF16)

    def ffn_half(x, l, k):
        wg = ffn_w_in[l, k][:, :D_FF].astype(BF16)
        wu = ffn_w_in[l, k][:, D_FF:].astype(BF16)
        return _ffn(x, mods[l], norm_w[l, 2 * k], wg, wu, ffn_w_out[l, k].astype(BF16), 6 * k)

    def mixer(x, l):
        w_tail = _pad_cols(w_in[l][:, N_MAIN:], TAIL_W).astype(BF16)
        proj, tail = _mixer_in(x, mods[l], norm_w[l, 1], w_in_bf, w_tail, l)
        tail_t = tail.reshape(N_TOK // GD_C, GD_C, TAIL_W).transpose(0, 2, 1)
        qn, kn, vc = _gdn_pre(proj, gd_conv_w[l])
        arow = _lane_vec(gd_A_log[l])
        drow = _lane_vec(gd_dt_bias[l])
        gd_args = (qn, kn, vc, tail, tail_t, arow.reshape(1, TAIL_W), drow.reshape(1, TAIL_W),
                   arow.reshape(TAIL_W, 1), drow.reshape(TAIL_W, 1))
        o_hg, o_gd, fin_hg, fin_gd = [], [], [], []
        for d in range(2):
            lb = lbs[d, l].reshape(1, GROUP_W)
            o, fin = _hgrn2_scan(proj, lb, jnp.log(lb), state_hgrn2, layer=l, reverse=d == 1)
            o_hg.append(o)
            fin_hg.append(fin)
            o, fin = _gdn_scan(*gd_args, state_gdn, layer=l, reverse=d == 1)
            o_gd.append(o)
            fin_gd.append(fin)
        x = _mixer_out(x, mods[l], o_hg[0], o_hg[1], o_gd[0], o_gd[1], proj,
                       hg_norm_w[l], gd_norm_w[l], w_out[l].astype(BF16))
        return x, jnp.stack(fin_hg, axis=1), jnp.stack(fin_gd, axis=1)

    new_hg, new_gd = [], []
    for l in range(DEPTH):
        x = ffn_half(x, l, 0)
        x, s_hg, s_gd = mixer(x, l)
        new_hg.append(s_hg)
        new_gd.append(s_gd)
        x = ffn_half(x, l, 1)

    y_prompt = _final_norm(x, final_norm_w, 0, N_PROMPT).reshape(BATCH, SEQ, D_MODEL)
    y_sample = _final_norm(x, final_norm_w, N_PROMPT, N_SAMPLE).reshape(DEC_BATCH, DEC_SEQ, D_MODEL)
    return (y_prompt, y_sample, jnp.stack(new_hg, axis=1), jnp.stack(new_gd, axis=1))
```

```python
import functools

import jax
import jax.numpy as jnp
from jax import lax
from jax.experimental import pallas as pl
from jax.experimental.pallas import tpu as pltpu

F32 = jnp.float32
BF16 = jnp.bfloat16

D_MODEL = 2048
BATCH = 32
SEQ = 256
DEPTH = 2
DEC_BATCH = 8
DEC_SEQ = 2048
GRID_W = 64
HEADS = 8
HEAD_DIM = 128
GROUP_W = HEADS * HEAD_DIM
CONV_K = 5
D_FF = 5504
N_MOD = 9
EPS = 1e-6

N_PROMPT = BATCH * SEQ
N_SAMPLE = DEC_BATCH * DEC_SEQ
N_TOK = N_PROMPT + N_SAMPLE
N_COND = 16
N_MAIN = 9 * GROUP_W
TAIL_W = 128

FF_TILE = 512
FFN_TM = 512
FFN_ROWS = 256
IN_TM = 1024
IN_ROWS = 256
IN_TN = 1024
OUT_TM = 256
NORM_TM = 1024
MOD_TN = 1024
SCAN_TB = 256
HG_C = 32
HG_GROUP = 4
GD_C = 64
GD_TB = 256
EXP_CLAMP = 80.0
VMEM_LIMIT = 52 * 1024 * 1024


def _mod_row(i, tm):
    n_ctx = N_PROMPT // tm
    return jnp.where(i < n_ctx, 0, 1 + (i - n_ctx) // (DEC_SEQ // tm))


def _sigmoid(x):
    return jax.nn.sigmoid(x)


def _softplus(x):
    return jnp.maximum(x, 0.0) + jnp.log1p(jnp.exp(-jnp.abs(x)))


def _dot(a, b):
    return jnp.dot(a, b, preferred_element_type=F32)


def _dot_nt(a, b):
    return lax.dot_general(a, b, (((1,), (1,)), ((), ())), preferred_element_type=F32)


def _dot_tn(a, b):
    return lax.dot_general(a, b, (((0,), (0,)), ((), ())), preferred_element_type=F32)


def _split3_bf16(x):
    hi = x.astype(BF16)
    r1 = x - hi.astype(F32)
    mid = r1.astype(BF16)
    return hi, mid, (r1 - mid.astype(F32)).astype(BF16)


def _dot_exact_lhs(a, x):
    hi, mid, lo = _split3_bf16(x)
    return _dot(a, hi) + (_dot(a, mid) + _dot(a, lo))


def _dot_exact_rhs(x, a):
    hi, mid, lo = _split3_bf16(x)
    return _dot(hi, a) + (_dot(mid, a) + _dot(lo, a))


def _each(f, *lists):
    return [f(*args) for args in zip(*lists)]


def _norm_modulate(x, nw, scale, shift):
    ms = jnp.mean(x * x, axis=-1, keepdims=True)
    return x * lax.rsqrt(ms + EPS) * nw * (1.0 + scale) + shift


def _mod_kernel(c_ref, w_ref, b_ref, o_ref):
    c = c_ref[...]
    s = (c * _sigmoid(c)).astype(BF16)
    o_ref[0] = _dot(s, w_ref[0].astype(BF16)) + b_ref[0]


def _modulation(cond, w_mod, b_mod):
    n_out = N_MOD * D_MODEL
    out = pl.pallas_call(
        _mod_kernel,
        grid=(DEPTH, n_out // MOD_TN),
        in_specs=[
            pl.BlockSpec((N_COND, D_MODEL), lambda l, j: (0, 0)),
            pl.BlockSpec((1, D_MODEL, MOD_TN), lambda l, j: (l, 0, j)),
            pl.BlockSpec((1, 1, MOD_TN), lambda l, j: (l, 0, j)),
        ],
        out_specs=pl.BlockSpec((1, N_COND, MOD_TN), lambda l, j: (l, 0, j)),
        out_shape=jax.ShapeDtypeStruct((DEPTH, N_COND, n_out), F32),
        name="modulation",
        compiler_params=pltpu.CompilerParams(
            dimension_semantics=("arbitrary", "arbitrary"), vmem_limit_bytes=VMEM_LIMIT),
    )(cond, w_mod, b_mod.reshape(DEPTH, 1, n_out))
    return out.reshape(DEPTH, N_COND, N_MOD, D_MODEL)


def _ffn_kernel(*refs, i_mod, nj, split_x):
    j = pl.program_id(1)
    rows = [slice(r * FFN_ROWS, (r + 1) * FFN_ROWS) for r in range(FFN_TM // FFN_ROWS)]
    if split_x:
        xc_ref, xl_ref, mod_ref, nw_ref, wg_ref, wu_ref, w2_ref, o_ref, hn_ref, acc_ref = refs
        is_ctx = pl.program_id(0) < N_PROMPT // FFN_TM
        x_rows = lambda r: jnp.where(is_ctx, xc_ref[r, :], xl_ref[r, :])
    else:
        x_ref, mod_ref, nw_ref, wg_ref, wu_ref, w2_ref, o_ref, hn_ref, acc_ref = refs
        x_rows = lambda r: x_ref[r, :]

    def accumulate(width, first=False):
        g, u = [], []
        for r in rows:
            if first:
                h = _norm_modulate(x_rows(r), nw_ref[...], mod_ref[0, i_mod + 1:i_mod + 2, :],
                                   mod_ref[0, i_mod:i_mod + 1, :]).astype(BF16)
                hn_ref[r, :] = h
            else:
                h = hn_ref[r, :]
            g.append(_dot(h, wg_ref[:, :width]))
            u.append(_dot(h, wu_ref[:, :width]))
        a = [(g_ * _sigmoid(g_) * u_).astype(BF16) for g_, u_ in zip(g, u)]
        for r, a_ in zip(rows, a):
            d = _dot(a_, w2_ref[:width, :])
            if first:
                acc_ref[r, :] = d
            else:
                acc_ref[r, :] += d

    @pl.when(j == 0)
    def _():
        accumulate(FF_TILE, first=True)

    @pl.when((j > 0) & (j < nj - 1))
    def _():
        accumulate(FF_TILE)

    @pl.when(j == nj - 1)
    def _():
        accumulate(D_FF - (nj - 1) * FF_TILE)
        for r in rows:
            o_ref[r, :] = x_rows(r) + 0.5 * mod_ref[0, i_mod + 2:i_mod + 3, :] * acc_ref[r, :]


def _ffn(x, mod, nw, wg, wu, w2, i_mod):
    nj = pl.cdiv(D_FF, FF_TILE)
    split_x = isinstance(x, tuple)
    if split_x:
        n_ctx = N_PROMPT // FFN_TM
        x_specs = [pl.BlockSpec((FFN_TM, D_MODEL), lambda i, j: (jnp.minimum(i, n_ctx - 1), 0)),
                   pl.BlockSpec((FFN_TM, D_MODEL), lambda i, j: (jnp.maximum(i - n_ctx, 0), 0))]
    else:
        x, x_specs = (x,), [pl.BlockSpec((FFN_TM, D_MODEL), lambda i, j: (i, 0))]
    return pl.pallas_call(
        functools.partial(_ffn_kernel, i_mod=i_mod, nj=nj, split_x=split_x),
        grid=(N_TOK // FFN_TM, nj),
        in_specs=x_specs + [
            pl.BlockSpec((1, N_MOD, D_MODEL), lambda i, j: (_mod_row(i, FFN_TM), 0, 0)),
            pl.BlockSpec((1, D_MODEL), lambda i, j: (0, 0)),
            pl.BlockSpec((D_MODEL, FF_TILE), lambda i, j: (0, j)),
            pl.BlockSpec((D_MODEL, FF_TILE), lambda i, j: (0, j)),
            pl.BlockSpec((FF_TILE, D_MODEL), lambda i, j: (j, 0)),
        ],
        out_specs=pl.BlockSpec((FFN_TM, D_MODEL), lambda i, j: (i, 0)),
        out_shape=jax.ShapeDtypeStruct((N_TOK, D_MODEL), F32),
        scratch_shapes=[pltpu.VMEM((FFN_TM, D_MODEL), BF16), pltpu.VMEM((FFN_TM, D_MODEL), F32)],
        name="ffn",
        compiler_params=pltpu.CompilerParams(
            dimension_semantics=("parallel", "arbitrary"), vmem_limit_bytes=VMEM_LIMIT),
    )(*x, mod, nw.reshape(1, D_MODEL), wg, wu, w2)


def _mixer_in_kernel(x_ref, mod_ref, nw_ref, w_ref, wt_ref, o_ref, t_ref, hn_ref):
    j = pl.program_id(1)

    @pl.when(j == 0)
    def _():
        for r0 in range(0, IN_TM, IN_ROWS):
            r = slice(r0, r0 + IN_ROWS)
            h = _norm_modulate(x_ref[r, :], nw_ref[...], mod_ref[0, 4:5, :], mod_ref[0, 3:4, :]).astype(BF16)
            hn_ref[r, :] = h
            o_ref[r, :] = _dot(h, w_ref[...])
            t_ref[r, :] = _dot(h, wt_ref[...])

    @pl.when(j > 0)
    def _():
        o_ref[...] = _dot(hn_ref[...], w_ref[...])


def _mixer_in(x, mod, nw, w_main, w_tail):
    return pl.pallas_call(
        _mixer_in_kernel,
        grid=(N_TOK // IN_TM, N_MAIN // IN_TN),
        in_specs=[
            pl.BlockSpec((IN_TM, D_MODEL), lambda i, j: (i, 0)),
            pl.BlockSpec((1, N_MOD, D_MODEL), lambda i, j: (_mod_row(i, IN_TM), 0, 0)),
            pl.BlockSpec((1, D_MODEL), lambda i, j: (0, 0)),
            pl.BlockSpec((D_MODEL, IN_TN), lambda i, j: (0, j)),
            pl.BlockSpec((D_MODEL, TAIL_W), lambda i, j: (0, 0)),
        ],
        out_specs=[
            pl.BlockSpec((IN_TM, IN_TN), lambda i, j: (i, j)),
            pl.BlockSpec((IN_TM, TAIL_W), lambda i, j: (i, 0)),
        ],
        out_shape=[jax.ShapeDtypeStruct((N_TOK, N_MAIN), F32),
                   jax.ShapeDtypeStruct((N_TOK, TAIL_W), F32)],
        scratch_shapes=[pltpu.VMEM((IN_TM, D_MODEL), BF16)],
        name="mixer_in",
        compiler_params=pltpu.CompilerParams(
            dimension_semantics=("parallel", "arbitrary"), vmem_limit_bytes=VMEM_LIMIT),
    )(x, mod, nw.reshape(1, D_MODEL), w_main, w_tail)


def _seq_pos(g, n_ctx_blk, ctx_len, lat_len, reverse):
    is_ctx = g < n_ctx_blk
    gl = jnp.maximum(g - n_ctx_blk, 0)
    seq = jnp.where(is_ctx, g // ctx_len, gl // lat_len)
    pos = jnp.where(is_ctx, g % ctx_len, gl % lat_len)
    n = jnp.where(is_ctx, ctx_len, lat_len)
    base = jnp.where(is_ctx, seq * ctx_len, n_ctx_blk + seq * lat_len)
    blk = base + ((n - 1 - pos) if reverse else pos)
    return is_ctx, seq, pos, n, blk


def _tri_mask(n, reverse):
    row = lax.broadcasted_iota(jnp.int32, (n, n), 0)
    col = lax.broadcasted_iota(jnp.int32, (n, n), 1)
    return (row <= col) if reverse else (row >= col)


def _hgrn2_kernel(q_ref, v_ref, f_ref, lb_ref, llb_ref, s0_ref, o_ref, fin_ref, st_ref, *, reverse):
    c = HG_C
    nch = SCAN_TB // c
    is_ctx, _, pos, seq_len, _ = _seq_pos(pl.program_id(0), N_PROMPT // SCAN_TB, SEQ // SCAN_TB,
                                          DEC_SEQ // SCAN_TB, reverse)

    @pl.when(pos == 0)
    def _():
        for h in range(HEADS):
            st_ref[h] = jnp.where(is_ctx, 0.0, s0_ref[0, 0, 0, h].T)

    incl = _tri_mask(c, reverse)
    tri = jnp.where(incl, 1.0, 0.0).astype(BF16)
    mid = c // 2
    hcs = [slice(h * HEAD_DIM, (h + 1) * HEAD_DIM) for h in range(HEADS)]

    lb = lb_ref[...]
    llb = llb_ref[...]
    l1mlb = jnp.log1p(-lb)
    omlb = 1.0 - lb

    def gates(rows):
        fr = f_ref[rows, :]
        qr = q_ref[rows, :]
        e = jnp.exp(-jnp.abs(fr))
        d = 1.0 + e
        b = l1mlb + jnp.minimum(fr, 0.0) - jnp.log(d)
        m = jnp.maximum(llb, b)
        logf = m + jnp.log(jnp.exp(llb - m) + jnp.exp(b - m))
        k = omlb * jnp.where(fr >= 0.0, e, 1.0) / d
        q = qr * _sigmoid(qr) * (HEAD_DIM ** -0.5)
        return logf, k, q

    def scaled(logf, k, q, bc):
        btot = jnp.sum(logf, axis=0, keepdims=True)
        r = bc[mid:mid + 1, :]
        q0 = (q * jnp.exp(bc)).astype(BF16)
        qm = (q * jnp.exp(jnp.minimum(bc - r, EXP_CLAMP))).astype(BF16)
        km = (k * jnp.exp(jnp.minimum(r - bc, EXP_CLAMP))).astype(BF16)
        kl = (k * jnp.exp(btot - bc)).astype(BF16)
        return q0, qm, km, kl, jnp.exp(btot)

    order = list(reversed(range(nch))) if reverse else list(range(nch))
    st = [st_ref[h] for h in range(HEADS)]
    for g0 in range(0, nch, HG_GROUP):
        rows = [slice(nn * c, (nn + 1) * c) for nn in order[g0:g0 + HG_GROUP]]
        gate = [gates(r) for r in rows]
        bc = [_dot_exact_lhs(tri, gt[0]) for gt in gate]
        sc = [scaled(*gt, b_) for gt, b_ in zip(gate, bc)]
        vb = [v_ref[r, :].astype(BF16) for r in rows]
        att = [[jnp.where(incl, _dot_nt(s_[1][:, hc], s_[2][:, hc]), 0.0).astype(BF16) for hc in hcs] for s_ in sc]
        intra = [[_dot(a_[h], v_[:, hcs[h]]) for h in range(HEADS)] for a_, v_ in zip(att, vb)]
        upd = [[_dot_tn(v_[:, hc], s_[3][:, hc]) for hc in hcs] for s_, v_ in zip(sc, vb)]
        for i, r in enumerate(rows):
            q0, dec = sc[i][0], sc[i][4]
            inter = [_dot_nt(q0[:, hcs[h]], st[h].astype(BF16)) for h in range(HEADS)]
            for h in range(HEADS):
                o_ref[r, hcs[h]] = intra[i][h] + inter[h]
            st = [st[h] * dec[:, hcs[h]] + upd[i][h] for h in range(HEADS)]
    for h in range(HEADS):
        st_ref[h] = st[h]

    @pl.when(pos == seq_len - 1)
    def _():
        for h in range(HEADS):
            fin_ref[0, h] = st_ref[h].T


def _hgrn2_scan(proj, lb, llb, state, *, layer, reverse):
    d = 1 if reverse else 0
    n_ctx_blk, ctx_len, lat_len = N_PROMPT // SCAN_TB, SEQ // SCAN_TB, DEC_SEQ // SCAN_TB
    sp = lambda g: _seq_pos(g, n_ctx_blk, ctx_len, lat_len, reverse)
    tok = lambda col: pl.BlockSpec((SCAN_TB, GROUP_W), lambda g: (sp(g)[4], col))
    vec = pl.BlockSpec((1, GROUP_W), lambda g: (0, 0))
    st_blk = (1, HEADS, HEAD_DIM, HEAD_DIM)
    o, fin = pl.pallas_call(
        functools.partial(_hgrn2_kernel, reverse=reverse),
        name="hgrn2_scan",
        grid=(N_TOK // SCAN_TB,),
        in_specs=[tok(0), tok(1), tok(2 + d), vec, vec,
                  pl.BlockSpec((1, 1) + st_blk, lambda g: (jnp.where(sp(g)[0], 0, sp(g)[1]), layer, d, 0, 0, 0))],
        out_specs=[tok(0),
                   pl.BlockSpec(st_blk, lambda g: (jnp.where(sp(g)[0], sp(g)[1], BATCH), 0, 0, 0))],
        out_shape=[jax.ShapeDtypeStruct((N_TOK, GROUP_W), F32),
                   jax.ShapeDtypeStruct((BATCH + 1,) + st_blk[1:], F32)],
        scratch_shapes=[pltpu.VMEM((HEADS, HEAD_DIM, HEAD_DIM), F32)],
        compiler_params=pltpu.CompilerParams(dimension_semantics=("arbitrary",), vmem_limit_bytes=VMEM_LIMIT),
    )(proj, proj, proj, lb, llb, state)
    return o, fin[:BATCH]


def _gdn_pre_kernel(q_ref, k_ref, v_ref, cwq_ref, cwk_ref, cwv_ref, qo_ref, ko_ref, vo_ref):
    i = pl.program_id(0)
    period = jnp.where(i < N_PROMPT // SCAN_TB, SCAN_TB, GRID_W)
    pos = lax.broadcasted_iota(jnp.int32, (SCAN_TB, GROUP_W), 0) & (period - 1)
    offs = [j - CONV_K // 2 for j in range(CONV_K)]
    valid = [(pos + off >= 0) & (pos + off < period) for off in offs]

    def conv_silu(x_ref, cw_ref):
        x = x_ref[...]
        acc = x * cw_ref[CONV_K // 2:CONV_K // 2 + 1, :]
        for j, off in enumerate(offs):
            if off == 0:
                continue
            xs = pltpu.roll(x, (-off) % SCAN_TB, 0)
            acc = acc + jnp.where(valid[j], xs, 0.0) * cw_ref[j:j + 1, :]
        return acc * _sigmoid(acc)

    def l2norm_heads(y, o_ref, scale):
        for h in range(HEADS):
            hc = slice(h * HEAD_DIM, (h + 1) * HEAD_DIM)
            yh = y[:, hc]
            o_ref[:, hc] = yh * (lax.rsqrt(jnp.sum(yh * yh, axis=-1, keepdims=True) + EPS) * scale)

    l2norm_heads(conv_silu(q_ref, cwq_ref), qo_ref, HEAD_DIM ** -0.5)
    l2norm_heads(conv_silu(k_ref, cwk_ref), ko_ref, 1.0)
    vo_ref[...] = conv_silu(v_ref, cwv_ref)


def _gdn_pre(proj, conv_w):
    cw = [conv_w[:, i * GROUP_W:(i + 1) * GROUP_W] for i in range(3)]
    tok = lambda col: pl.BlockSpec((SCAN_TB, GROUP_W), lambda i: (i, col))
    cws = pl.BlockSpec((CONV_K, GROUP_W), lambda i: (0, 0))
    out = jax.ShapeDtypeStruct((N_TOK, GROUP_W), F32)
    return pl.pallas_call(
        _gdn_pre_kernel,
        grid=(N_TOK // SCAN_TB,),
        in_specs=[tok(5), tok(6), tok(7), cws, cws, cws],
        out_specs=[tok(0), tok(0), tok(0)],
        out_shape=[out, out, out],
        name="gdn_pre",
        compiler_params=pltpu.CompilerParams(
            dimension_semantics=("parallel",), vmem_limit_bytes=VMEM_LIMIT),
    )(proj, proj, proj, *cw)


def _gdn_kernel(q_ref, k_ref, v_ref, ab_ref, abt_ref, arow_ref, drow_ref, acol_ref, dcol_ref, s0_ref,
                o_ref, fin_ref, s_ref, *, reverse, d):
    c = GD_C
    ncs = GD_TB // c
    is_ctx, _, pos, seq_len, _ = _seq_pos(pl.program_id(0), N_PROMPT // GD_TB, SEQ // GD_TB,
                                          DEC_SEQ // GD_TB, reverse)

    @pl.when(pos == 0)
    def _():
        for h in range(HEADS):
            s_ref[h] = jnp.where(is_ctx, 0.0, s0_ref[0, 0, 0, h])

    incl = _tri_mask(c, reverse)
    incl_t = _tri_mask(c, not reverse)
    row = lax.broadcasted_iota(jnp.int32, (c, c), 0)
    col = lax.broadcasted_iota(jnp.int32, (c, c), 1)
    strict = incl & (row != col)
    eye = (row == col).astype(F32)
    same_blk = {lvl: (row >> lvl) == (col >> lvl) for lvl in range(3, c.bit_length())}
    tri = jnp.where(incl, 1.0, 0.0).astype(BF16)
    tri_t = jnp.where(incl_t, 1.0, 0.0).astype(BF16)
    last = 0 if reverse else c - 1
    rows = [slice(ci * c, (ci + 1) * c) for ci in range(ncs)]

    gcum, gcum_t, beta_tok = [], [], []
    for ci in range(ncs):
        ab = ab_ref[rows[ci], :]
        gcum.append(_dot_exact_lhs(tri, -jnp.exp(arow_ref[...]) * _softplus(ab + drow_ref[...])))
        g_head = -jnp.exp(acol_ref[...]) * _softplus(abt_ref[ci] + dcol_ref[...])
        gcum_t.append(_dot_exact_rhs(g_head, tri_t))
        beta_tok.append(_sigmoid(ab))

    hcs = [slice(h * HEAD_DIM, (h + 1) * HEAD_DIM) for h in range(HEADS)]
    prob = [(ci, h) for ci in range(ncs) for h in range(HEADS)]
    ia = [d * HEADS + h for h in range(HEADS)]
    ib = [2 * HEADS + d * HEADS + h for h in range(HEADS)]
    bf = lambda xs: [x.astype(BF16) for x in xs]
    qn = [q_ref[rows[ci], hcs[h]] for ci, h in prob]
    kn = [k_ref[rows[ci], hcs[h]] for ci, h in prob]
    vv = [v_ref[rows[ci], hcs[h]] for ci, h in prob]
    gcol = [gcum[ci][:, ia[h]:ia[h] + 1] for ci, h in prob]
    grow = [gcum_t[ci][ia[h]:ia[h] + 1, :] for ci, h in prob]
    bcol = [beta_tok[ci][:, ib[h]:ib[h] + 1] for ci, h in prob]
    glast = [gcum[ci][last:last + 1, ia[h]:ia[h] + 1] for ci, h in prob]
    decay = _each(lambda gc, gr: jnp.exp(jnp.where(incl, gc - gr, -1e30)), gcol, grow)
    knb = bf(kn)
    kb = _each(jnp.multiply, kn, bcol)
    kk = _each(_dot_nt, bf(kb), knb)
    a = _each(lambda x, dc: jnp.where(strict, x * dc, 0.0), kk, decay)
    bdot = lambda xs, ys: _each(_dot, bf(xs), bf(ys))
    p = [jnp.where(same_blk[3], -x, 0.0) for x in a]
    t = [eye + x for x in p]
    p = bdot(p, p)
    t = _each(jnp.add, t, bdot(t, p))
    p = bdot(p, p)
    t = _each(jnp.add, t, bdot(t, p))
    for lvl in range(3, GD_C.bit_length() - 1):
        couple = same_blk[lvl + 1] & jnp.logical_not(same_blk[lvl])
        n = [jnp.where(couple, x, 0.0) for x in a]
        t = _each(jnp.subtract, t, bdot(bdot(t, n), t))
    rhs = _each(lambda v_, b_, kb_, gc: jnp.concatenate([v_ * b_, kb_ * jnp.exp(gc)], axis=-1), vv, bcol, kb, gcol)
    sol = bdot(t, rhs)
    qk = _each(_dot_nt, bf(qn), knb)
    aqk = bf(_each(jnp.multiply, qk, decay))
    u = [x[:, :HEAD_DIM] for x in sol]
    wb = bf([x[:, HEAD_DIM:] for x in sol])
    qe = bf(_each(lambda q_, gc: q_ * jnp.exp(gc), qn, gcol))
    kdec = bf(_each(lambda k_, gl, gc: k_ * jnp.exp(gl - gc), kn, glast, gcol))
    sdec = [jnp.exp(x) for x in glast]

    s = [s_ref[h] for h in range(HEADS)]
    for ci in (reversed(range(ncs)) if reverse else range(ncs)):
        idx = [ci * HEADS + h for h in range(HEADS)]
        sb = bf(s)
        ws = [_dot(wb[i], sb[h]) for h, i in enumerate(idx)]
        vnb = bf([u[i] - ws[h] for h, i in enumerate(idx)])
        qs = [_dot(qe[i], sb[h]) for h, i in enumerate(idx)]
        av = [_dot(aqk[i], vnb[h]) for h, i in enumerate(idx)]
        kv = [_dot_tn(kdec[i], vnb[h]) for h, i in enumerate(idx)]
        for h, i in enumerate(idx):
            o_ref[rows[ci], hcs[h]] = qs[h] + av[h]
        s = [s[h] * sdec[i] + kv[h] for h, i in enumerate(idx)]
    for h in range(HEADS):
        s_ref[h] = s[h]

    @pl.when(pos == seq_len - 1)
    def _():
        for h in range(HEADS):
            fin_ref[0, h] = s_ref[h]


def _gdn_scan(qn, kn, vc, tail, tail_t, arow, drow, acol, dcol, state, *, layer, reverse):
    d = 1 if reverse else 0
    sp = lambda g: _seq_pos(g, N_PROMPT // GD_TB, SEQ // GD_TB, DEC_SEQ // GD_TB, reverse)
    tok = pl.BlockSpec((GD_TB, GROUP_W), lambda g: (sp(g)[4], 0))
    vec_row = pl.BlockSpec((1, TAIL_W), lambda g: (0, 0))
    vec_col = pl.BlockSpec((TAIL_W, 1), lambda g: (0, 0))
    st_blk = (1, HEADS, HEAD_DIM, HEAD_DIM)
    o, fin = pl.pallas_call(
        functools.partial(_gdn_kernel, reverse=reverse, d=d),
        name="gdn_scan",
        grid=(N_TOK // GD_TB,),
        in_specs=[tok, tok, tok,
                  pl.BlockSpec((GD_TB, TAIL_W), lambda g: (sp(g)[4], 0)),
                  pl.BlockSpec((GD_TB // GD_C, TAIL_W, GD_C), lambda g: (sp(g)[4], 0, 0)),
                  vec_row, vec_row, vec_col, vec_col,
                  pl.BlockSpec((1, 1) + st_blk, lambda g: (jnp.where(sp(g)[0], 0, sp(g)[1]), layer, d, 0, 0, 0))],
        out_specs=[tok,
                   pl.BlockSpec(st_blk, lambda g: (jnp.where(sp(g)[0], sp(g)[1], BATCH), 0, 0, 0))],
        out_shape=[jax.ShapeDtypeStruct((N_TOK, GROUP_W), F32),
                   jax.ShapeDtypeStruct((BATCH + 1,) + st_blk[1:], F32)],
        scratch_shapes=[pltpu.VMEM((HEADS, HEAD_DIM, HEAD_DIM), F32)],
        compiler_params=pltpu.CompilerParams(dimension_semantics=("arbitrary",), vmem_limit_bytes=VMEM_LIMIT),
    )(qn, kn, vc, tail, tail_t, arow, drow, acol, dcol, state)
    return o, fin[:BATCH]


def _mixer_out_kernel(x_ref, mod_ref, ohf_ref, ohb_ref, ogf_ref, ogb_ref, hgate_ref, ggate_ref,
                      hw_ref, gw_ref, wo_ref, o_ref, cat_ref):
    def norm_gate(of_ref, ob_ref, gate_ref, w_ref, base):
        for h in range(HEADS):
            hc = slice(h * HEAD_DIM, (h + 1) * HEAD_DIM)
            o = of_ref[:, hc] + ob_ref[:, hc]
            on = o * lax.rsqrt(jnp.mean(o * o, axis=-1, keepdims=True) + EPS) * w_ref[...]
            gate = gate_ref[:, hc]
            cat_ref[:, base + h * HEAD_DIM:base + (h + 1) * HEAD_DIM] = (on * (gate * _sigmoid(gate))).astype(BF16)

    norm_gate(ohf_ref, ohb_ref, hgate_ref, hw_ref, 0)
    norm_gate(ogf_ref, ogb_ref, ggate_ref, gw_ref, GROUP_W)
    o_ref[...] = x_ref[...] + mod_ref[0, 5:6, :] * _dot(cat_ref[...], wo_ref[...])


def _mixer_out(x, mod, o_hg_f, o_hg_b, o_gd_f, o_gd_b, proj, hg_norm_w, gd_norm_w, w_out):
    tok = lambda col: pl.BlockSpec((OUT_TM, GROUP_W), lambda i: (i, col))
    vec = pl.BlockSpec((1, HEAD_DIM), lambda i: (0, 0))
    return pl.pallas_call(
        _mixer_out_kernel,
        grid=(N_TOK // OUT_TM,),
        in_specs=[
            pl.BlockSpec((OUT_TM, D_MODEL), lambda i: (i, 0)),
            pl.BlockSpec((1, N_MOD, D_MODEL), lambda i: (_mod_row(i, OUT_TM), 0, 0)),
            tok(0), tok(0), tok(0), tok(0), tok(4), tok(8),
            vec, vec,
            pl.BlockSpec((D_MODEL, D_MODEL), lambda i: (0, 0)),
        ],
        out_specs=pl.BlockSpec((OUT_TM, D_MODEL), lambda i: (i, 0)),
        out_shape=jax.ShapeDtypeStruct((N_TOK, D_MODEL), F32),
        scratch_shapes=[pltpu.VMEM((OUT_TM, D_MODEL), BF16)],
        name="mixer_out",
        compiler_params=pltpu.CompilerParams(
            dimension_semantics=("parallel",), vmem_limit_bytes=VMEM_LIMIT),
    )(x, mod, o_hg_f, o_hg_b, o_gd_f, o_gd_b, proj, proj,
      hg_norm_w.reshape(1, HEAD_DIM), gd_norm_w.reshape(1, HEAD_DIM), w_out)


def _final_norm_kernel(x_ref, w_ref, o_ref):
    x = x_ref[...]
    o_ref[...] = x * lax.rsqrt(jnp.mean(x * x, axis=-1, keepdims=True) + EPS) * w_ref[...]


def _final_norm(x, w, row0, n_rows):
    blk0 = row0 // NORM_TM
    return pl.pallas_call(
        _final_norm_kernel,
        grid=(n_rows // NORM_TM,),
        in_specs=[pl.BlockSpec((NORM_TM, D_MODEL), lambda i: (blk0 + i, 0)),
                  pl.BlockSpec((1, D_MODEL), lambda i: (0, 0))],
        out_specs=pl.BlockSpec((NORM_TM, D_MODEL), lambda i: (i, 0)),
        out_shape=jax.ShapeDtypeStruct((n_rows, D_MODEL), F32),
        name="final_norm",
        compiler_params=pltpu.CompilerParams(
            dimension_semantics=("parallel",), vmem_limit_bytes=VMEM_LIMIT),
    )(x, w.reshape(1, D_MODEL))


def _pad_cols(w, n):
    return jnp.pad(w, ((0, 0), (0, n - w.shape[1])))


def _lane_vec(v):
    return jnp.pad(v.reshape(-1).astype(F32), (0, TAIL_W - 2 * HEADS))


def kernel(x_prompt, x_sample, c, state_hgrn2, state_gdn, c_ctx, norm_w, w_mod, b_mod, ffn_w_in, ffn_w_out,
           w_in, hg_lower_bounds, hg_norm_w, gd_conv_w, gd_A_log, gd_dt_bias, gd_norm_w, w_out, final_norm_w):
    x = (x_prompt.reshape(N_PROMPT, D_MODEL), x_sample.reshape(N_SAMPLE, D_MODEL))
    cond = jnp.concatenate([c_ctx[None, :], c, jnp.zeros((N_COND - 1 - DEC_BATCH, D_MODEL), F32)], axis=0)
    mods = _modulation(cond, w_mod, b_mod)

    cs = jnp.cumsum(jax.nn.softmax(hg_lower_bounds.astype(F32), axis=1), axis=1)
    lbs = cs - cs[:, :1]

    def ffn_half(x, l, k):
        wg = ffn_w_in[l, k][:, :D_FF].astype(BF16)
        wu = ffn_w_in[l, k][:, D_FF:].astype(BF16)
        return _ffn(x, mods[l], norm_w[l, 2 * k], wg, wu, ffn_w_out[l, k].astype(BF16), 6 * k)

    def mixer(x, l):
        w_tail = _pad_cols(w_in[l][:, N_MAIN:], TAIL_W).astype(BF16)
        proj, tail = _mixer_in(x, mods[l], norm_w[l, 1], w_in[l][:, :N_MAIN].astype(BF16), w_tail)
        tail_t = tail.reshape(N_TOK // GD_C, GD_C, TAIL_W).transpose(0, 2, 1)
        qn, kn, vc = _gdn_pre(proj, gd_conv_w[l])
        arow = _lane_vec(gd_A_log[l])
        drow = _lane_vec(gd_dt_bias[l])
        gd_args = (qn, kn, vc, tail, tail_t, arow.reshape(1, TAIL_W), drow.reshape(1, TAIL_W),
                   arow.reshape(TAIL_W, 1), drow.reshape(TAIL_W, 1))
        o_hg, o_gd, fin_hg, fin_gd = [], [], [], []
        for d in range(2):
            lb = lbs[d, l].reshape(1, GROUP_W)
            o, fin = _hgrn2_scan(proj, lb, jnp.log(lb), state_hgrn2, layer=l, reverse=d == 1)
            o_hg.append(o)
            fin_hg.append(fin)
            o, fin = _gdn_scan(*gd_args, state_gdn, layer=l, reverse=d == 1)
            o_gd.append(o)
            fin_gd.append(fin)
        x = _mixer_out(x, mods[l], o_hg[0], o_hg[1], o_gd[0], o_gd[1], proj,
                       hg_norm_w[l], gd_norm_w[l], w_out[l].astype(BF16))
        return x, jnp.stack(fin_hg, axis=1), jnp.stack(fin_gd, axis=1)

    new_hg, new_gd = [], []
    for l in range(DEPTH):
        x = ffn_half(x, l, 0)
        x, s_hg, s_gd = mixer(x, l)
        new_hg.append(s_hg)
        new_gd.append(s_gd)
        x = ffn_half(x, l, 1)

    y_prompt = _final_norm(x, final_norm_w, 0, N_PROMPT).reshape(BATCH, SEQ, D_MODEL)
    y_sample = _final_norm(x, final_norm_w, N_PROMPT, N_SAMPLE).reshape(DEC_BATCH, DEC_SEQ, D_MODEL)
    return (y_prompt, y_sample, jnp.stack(new_hg, axis=1), jnp.stack(new_gd, axis=1))
```
